```python
import math
import jax, jax.numpy as jnp
from jax import lax
import numpy as np

D_MODEL = 1024
BATCH = 16
SEQ = 4096
DEPTH = 1
DEC_BATCH = 128
DEC_SEQ = 4
PAST_LEN = 8192
PAGE_SIZE = 128

H_D = 4
D_QK = 64
DV_D = 128
W_D = H_D * DV_D
H_R = 4
DK_R = 64
DV_R = 128
W_R = H_R * DV_R
MIX_WIDTH = W_D + W_R
PROJ_WIDTH = 2 * H_D * 2 * D_QK + 2 * W_D + 2 * H_R * DK_R + 2 * W_R
NUM_BUCKETS = 32
MAX_DISTANCE = 128
Q_BLOCK = 128
RET_CHUNK = 128
EPS = 1e-6
NEG = -1e30
ATTN_SCALE = D_QK ** -0.5

kernel_name = "hymba_diffattn_retention_decoder_step"


def _rmsnorm(x, g):
    xf = x.astype(jnp.float32)
    y = xf * lax.rsqrt(jnp.mean(xf * xf, axis=-1, keepdims=True) + EPS)
    return (y * g.astype(jnp.float32)).astype(x.dtype)


def _t5_bucket(rel):
    n = jnp.maximum(rel, 0)
    max_exact = NUM_BUCKETS // 2
    nf = jnp.maximum(n, max_exact).astype(jnp.float32)
    large = max_exact + (jnp.log(nf / max_exact) / math.log(MAX_DISTANCE / max_exact)
                         * (NUM_BUCKETS - max_exact)).astype(jnp.int32)
    large = jnp.minimum(large, NUM_BUCKETS - 1)
    return jnp.where(n < max_exact, n, large)


def _rel_bias(rel_bias, qpos, kpos):
    bucket = _t5_bucket(qpos[:, None] - kpos[None, :])
    return rel_bias[bucket].astype(jnp.float32).transpose(2, 0, 1)


def _split_proj(h, w_in):
    z = jnp.einsum('bsd,de->bse', h, w_in)
    b, s, _ = z.shape
    sizes = [H_D * 2 * D_QK, H_D * 2 * D_QK, W_D, W_D, H_R * DK_R, H_R * DK_R, W_R, W_R]
    off = [0] + [int(v) for v in np.cumsum(sizes)]
    p = [z[..., off[i]:off[i + 1]] for i in range(len(sizes))]
    qd = p[0].reshape(b, s, H_D, 2 * D_QK)
    kd = p[1].reshape(b, s, H_D, 2 * D_QK)
    vd = p[2].reshape(b, s, H_D, DV_D)
    gd = p[3]
    qr = p[4].reshape(b, s, H_R, DK_R)
    kr = p[5].reshape(b, s, H_R, DK_R)
    vr = p[6].reshape(b, s, H_R, DV_R)
    gr = p[7]
    return qd, kd, vd, gd, qr, kr, vr, gr


def _rotate(x, pos):
    half = x.shape[-1] // 2
    inv = 1.0 / (10000.0 ** jnp.linspace(0.0, 1.0, half, dtype=jnp.float32))
    ang = pos.astype(jnp.float32)[:, None] * inv[None, :]
    cos = jnp.cos(ang)[None, :, None, :]
    sin = jnp.sin(ang)[None, :, None, :]
    x1 = x[..., :half].astype(jnp.float32)
    x2 = x[..., half:].astype(jnp.float32)
    return jnp.concatenate([x1 * cos - x2 * sin, x1 * sin + x2 * cos], axis=-1)


def _diff_attn_prompt(qd, kd, vd, rel_bias, lam):
    b, s = qd.shape[:2]
    nb = s // Q_BLOCK
    qb = qd.astype(jnp.float32).reshape(b, nb, Q_BLOCK, H_D, 2 * D_QK).swapaxes(0, 1)
    k1 = kd[..., :D_QK].astype(jnp.float32)
    k2 = kd[..., D_QK:].astype(jnp.float32)
    vf = vd.astype(jnp.float32)
    kpos = jnp.arange(s, dtype=jnp.int32)

    def block(args):
        q, i = args
        qpos = i * Q_BLOCK + jnp.arange(Q_BLOCK, dtype=jnp.int32)
        bias = _rel_bias(rel_bias, qpos, kpos)
        mask = (qpos[:, None] - kpos[None, :]) >= 0

        def softmax_map(qc, kc):
            sc = jnp.einsum('bqhd,bkhd->bhqk', qc, kc) * ATTN_SCALE + bias
            return jax.nn.softmax(jnp.where(mask, sc, NEG), axis=-1)

        a = softmax_map(q[..., :D_QK], k1) - lam * softmax_map(q[..., D_QK:], k2)
        return jnp.einsum('bhqk,bkhe->bqhe', a, vf)

    o = lax.map(block, (qb, jnp.arange(nb, dtype=jnp.int32)))
    return o.swapaxes(0, 1).reshape(b, s, H_D, DV_D)


def _online_update(carry, q1, q2, k, v, bias, mask):
    k1 = k[..., :D_QK].astype(jnp.float32)
    k2 = k[..., D_QK:].astype(jnp.float32)
    vf = v.astype(jnp.float32)
    new = []
    for j, (qc, kc) in enumerate(((q1, k1), (q2, k2))):
        m, l, a = carry[3 * j], carry[3 * j + 1], carry[3 * j + 2]
        sc = jnp.einsum('bqhd,bkhd->bhqk', qc, kc) * ATTN_SCALE + bias
        sc = jnp.where(mask, sc, NEG)
        m_new = jnp.maximum(m, jnp.max(sc, axis=-1))
        corr = jnp.exp(m - m_new)
        p = jnp.exp(sc - m_new[..., None])
        l = l * corr + jnp.sum(p, axis=-1)
        a = a * corr[..., None] + jnp.einsum('bhqk,bkhe->bhqe', p, vf)
        new += [m_new, l, a]
    return tuple(new)


def _diff_attn_sample(qd, kd, vd, cache_k, cache_v, layer, page_table, rel_bias, lam):
    b, t = qd.shape[:2]
    n_pages = page_table.shape[1]
    page = cache_k.shape[2]
    past = n_pages * page
    qpos = past + jnp.arange(t, dtype=jnp.int32)
    q1 = qd[..., :D_QK].astype(jnp.float32)
    q2 = qd[..., D_QK:].astype(jnp.float32)
    m0 = jnp.full((b, H_D, t), NEG, jnp.float32)
    l0 = jnp.zeros((b, H_D, t), jnp.float32)
    a0 = jnp.zeros((b, H_D, t, DV_D), jnp.float32)
    init = (m0, l0, a0, m0, l0, a0)
    full_mask = jnp.ones((t, page), dtype=bool)

    def step(carry, p):
        phys = page_table[:, p]
        kp = cache_k[layer, phys]
        vp = cache_v[layer, phys]
        kpos = p * page + jnp.arange(page, dtype=jnp.int32)
        bias = _rel_bias(rel_bias, qpos, kpos)
        return _online_update(carry, q1, q2, kp, vp, bias, full_mask), None

    carry, _ = lax.scan(step, init, jnp.arange(n_pages, dtype=jnp.int32))
    bias = _rel_bias(rel_bias, qpos, qpos)
    mask = (qpos[:, None] - qpos[None, :]) >= 0
    m1, l1, a1, m2, l2, a2 = _online_update(carry, q1, q2, kd, vd, bias, mask)
    o = a1 / l1[..., None] - lam * (a2 / l2[..., None])
    return o.transpose(0, 2, 1, 3)


def _ret_log_gamma():
    return jnp.log(1.0 - 2.0 ** (-5.0 - jnp.arange(H_R, dtype=jnp.float32)))


def _retention_chunk(S, q, k, v, log_gamma):
    c = q.shape[1]
    idx = jnp.arange(c, dtype=jnp.float32)
    d = idx[:, None] - idx[None, :]
    decay = jnp.where(d[None] >= 0,
                      jnp.exp(log_gamma[:, None, None] * jnp.maximum(d, 0.0)[None]), 0.0)
    qf = q.astype(jnp.float32)
    kf = k.astype(jnp.float32)
    vf = v.astype(jnp.float32)
    scores = jnp.einsum('bihd,bjhd->bhij', qf, kf) * decay
    inner = jnp.einsum('bhij,bjhe->bihe', scores, vf)
    q_decay = jnp.exp(log_gamma[:, None] * (idx + 1.0)[None, :]).T
    cross = jnp.einsum('bihd,bhde->bihe', qf, S) * q_decay[None, :, :, None]
    k_decay = jnp.exp(log_gamma[:, None] * (c - 1.0 - idx)[None, :]).T
    S_new = (jnp.exp(log_gamma * c)[None, :, None, None] * S
             + jnp.einsum('bjhd,bjhe->bhde', kf * k_decay[None, :, :, None], vf))
    return S_new, inner + cross


def _retention_prompt(q, k, v, log_gamma):
    b, s = q.shape[:2]
    c = min(RET_CHUNK, s)
    nc = s // c

    def to_chunks(t):
        return t.reshape(b, nc, c, *t.shape[2:]).swapaxes(0, 1)

    S0 = jnp.zeros((b, H_R, DK_R, DV_R), jnp.float32)

    def step(S, xs):
        return _retention_chunk(S, xs[0], xs[1], xs[2], log_gamma)

    S, o = lax.scan(step, S0, (to_chunks(q), to_chunks(k), to_chunks(v)))
    return S, o.swapaxes(0, 1).reshape(b, s, H_R, DV_R)


def _merge_out(o_d, gd, o_r, gr, gn_diff, gn_ret, w_out, lambda_init, dtype):
    b, s = o_d.shape[:2]
    od = (_rmsnorm(o_d, gn_diff) * (1.0 - lambda_init)).reshape(b, s, W_D)
    orr = _rmsnorm(o_r, gn_ret).reshape(b, s, W_R)
    od = od * jax.nn.silu(gd.astype(jnp.float32))
    orr = orr * jax.nn.silu(gr.astype(jnp.float32))
    o = jnp.concatenate([od, orr], axis=-1).astype(dtype)
    return jnp.einsum('bse,ed->bsd', o, w_out).astype(dtype)


def setup_inputs(seed: int = 0) -> dict:
    key = jax.random.key(seed)
    ks = jax.random.split(key, 20)
    n_pages = PAST_LEN // PAGE_SIZE
    n_used = DEC_BATCH * n_pages
    n_pool = (n_used * 5) // 4
    f32 = jnp.float32
    x_prompt = jax.random.normal(ks[0], (BATCH, SEQ, D_MODEL), f32)
    x_sample = jax.random.normal(ks[1], (DEC_BATCH, DEC_SEQ, D_MODEL), f32)
    cache_k = jax.random.normal(ks[2], (DEPTH, n_pool, PAGE_SIZE, H_D, 2 * D_QK), f32)
    cache_v = jax.random.normal(ks[3], (DEPTH, n_pool, PAGE_SIZE, H_D, DV_D), f32)
    state_ret = 0.5 * jax.random.normal(ks[4], (DEPTH, DEC_BATCH, H_R, DK_R, DV_R), f32)
    page_table = jax.random.permutation(ks[5], n_pool)[:n_used].reshape(DEC_BATCH, n_pages).astype(jnp.int32)
    rel_bias = 0.5 * jax.random.normal(ks[6], (NUM_BUCKETS, H_D), f32)
    norm_pre = 1.0 + 0.1 * jax.random.normal(ks[7], (DEPTH, D_MODEL), f32)
    norm_post = 1.0 + 0.1 * jax.random.normal(ks[8], (DEPTH, D_MODEL), f32)
    w_in = jax.random.normal(ks[9], (DEPTH, D_MODEL, PROJ_WIDTH), f32) * D_MODEL ** -0.5
    w_out = jax.random.normal(ks[10], (DEPTH, MIX_WIDTH, D_MODEL), f32) * MIX_WIDTH ** -0.5
    lambda_q1 = 0.1 * jax.random.normal(ks[11], (DEPTH, D_QK), f32)
    lambda_k1 = 0.1 * jax.random.normal(ks[12], (DEPTH, D_QK), f32)
    lambda_q2 = 0.1 * jax.random.normal(ks[13], (DEPTH, D_QK), f32)
    lambda_k2 = 0.1 * jax.random.normal(ks[14], (DEPTH, D_QK), f32)
    gn_diff = 1.0 + 0.1 * jax.random.normal(ks[15], (DEPTH, DV_D), f32)
    gn_ret = 1.0 + 0.1 * jax.random.normal(ks[16], (DEPTH, DV_R), f32)
    return {"x_prompt": x_prompt, "x_sample": x_sample, "cache_k": cache_k, "cache_v": cache_v,
            "state_ret": state_ret, "page_table": page_table, "rel_bias": rel_bias,
            "norm_pre": norm_pre, "norm_post": norm_post, "w_in": w_in, "w_out": w_out,
            "lambda_q1": lambda_q1, "lambda_k1": lambda_k1, "lambda_q2": lambda_q2,
            "lambda_k2": lambda_k2, "gn_diff": gn_diff, "gn_ret": gn_ret}


def reference(x_prompt, x_sample, cache_k, cache_v, state_ret, page_table, rel_bias,
              norm_pre, norm_post, w_in, w_out, lambda_q1, lambda_k1, lambda_q2, lambda_k2,
              gn_diff, gn_ret):
    log_gamma = _ret_log_gamma()
    n_pages = page_table.shape[1]
    past = n_pages * cache_k.shape[2]
    yp, ys = x_prompt, x_sample
    kp_l, vp_l, sp_l, ks_l, vs_l, ss_l = [], [], [], [], [], []
    for layer in range(DEPTH):
        lambda_init = 0.8 - 0.6 * math.exp(-0.3 * layer)
        lam = (jnp.exp(jnp.sum(lambda_q1[layer].astype(jnp.float32) * lambda_k1[layer].astype(jnp.float32)))
               - jnp.exp(jnp.sum(lambda_q2[layer].astype(jnp.float32) * lambda_k2[layer].astype(jnp.float32)))
               + lambda_init)

        h = _rmsnorm(yp, norm_pre[layer])
        qd, kd, vd, gd, qr, kr, vr, gr = _split_proj(h, w_in[layer])
        pos = jnp.arange(yp.shape[1], dtype=jnp.int32)
        qr = _rotate(qr, pos)
        kr = _rotate(kr, pos) * DK_R ** -0.5
        o_d = _diff_attn_prompt(qd, kd, vd, rel_bias, lam)
        S_p, o_r = _retention_prompt(qr, kr, vr, log_gamma)
        mix = _merge_out(o_d, gd, o_r, gr, gn_diff[layer], gn_ret[layer], w_out[layer], lambda_init, yp.dtype)
        yp = yp + _rmsnorm(mix, norm_post[layer])
        kp_l.append(kd)
        vp_l.append(vd)
        sp_l.append(S_p)

        h = _rmsnorm(ys, norm_pre[layer])
        qd, kd, vd, gd, qr, kr, vr, gr = _split_proj(h, w_in[layer])
        pos = past + jnp.arange(ys.shape[1], dtype=jnp.int32)
        qr = _rotate(qr, pos)
        kr = _rotate(kr, pos) * DK_R ** -0.5
        o_d = _diff_attn_sample(qd, kd, vd, cache_k, cache_v, layer, page_table, rel_bias, lam)
        S_s, o_r = _retention_chunk(state_ret[layer].astype(jnp.float32), qr, kr, vr, log_gamma)
        mix = _merge_out(o_d, gd, o_r, gr, gn_diff[layer], gn_ret[layer], w_out[layer], lambda_init, ys.dtype)
        ys = ys + _rmsnorm(mix, norm_post[layer])
        ks_l.append(kd)
        vs_l.append(vd)
        ss_l.append(S_s)

    return (yp, ys, jnp.stack(kp_l), jnp.stack(vp_l), jnp.stack(sp_l),
            jnp.stack(ks_l), jnp.stack(vs_l), jnp.stack(ss_l))
```

```python
import functools
import math

import jax
import jax.numpy as jnp
import numpy as np
from jax import lax
from jax.experimental import pallas as pl
from jax.experimental.pallas import tpu as pltpu

F32 = jnp.float32
BF16 = jnp.bfloat16

H_D = 4
D_QK = 64
DV_D = 128
W_D = H_D * DV_D
H_R = 4
DK_R = 64
DV_R = 128
W_R = H_R * DV_R
NUM_BUCKETS = 32
MAX_DISTANCE = 128
RET_CHUNK = 128
EPS = 1e-6
NEG = -1e30
LOG2E = 1.4426950408889634
Q_PRESCALE = D_QK ** -0.5 * LOG2E
MASK_CODE = NUM_BUCKETS
LANES = 128
VMEM_LIMIT = 56 * 1024 * 1024

_SIZES = [H_D * 2 * D_QK, H_D * 2 * D_QK, W_D, W_D, H_R * DK_R, H_R * DK_R, W_R, W_R]
_OFF = [0] + [int(v) for v in np.cumsum(_SIZES)]


def _cparams(n_axes):
    return pltpu.CompilerParams(dimension_semantics=("arbitrary",) * n_axes,
                                vmem_limit_bytes=VMEM_LIMIT)


def _bucket_np(rel):
    n = np.maximum(rel, 0)
    max_exact = NUM_BUCKETS // 2
    nf = np.maximum(n, max_exact).astype(np.float32)
    large = max_exact + (np.log(nf / np.float32(max_exact)) / np.float32(math.log(MAX_DISTANCE / max_exact))
                         * np.float32(NUM_BUCKETS - max_exact)).astype(np.int32)
    large = np.minimum(large, NUM_BUCKETS - 1)
    return np.where(n < max_exact, n, large).astype(np.int32)


def _far_distance():
    b = _bucket_np(np.arange(4 * MAX_DISTANCE))
    return int(np.max(np.nonzero(b != NUM_BUCKETS - 1)[0])) + 1


def _prompt_bucket_codes(t):
    r = np.arange(t)[:, None]
    c = np.arange(t)[None, :]
    diag = np.where(r >= c, _bucket_np(r - c), MASK_CODE)
    left = _bucket_np(r - c + t)
    return np.stack([diag, left]).astype(np.int32)


def _decode_bucket_codes(page, t_new):
    t = np.tile(np.arange(t_new), 2)[:, None]
    j = np.arange(page)[None, :]
    last = _bucket_np(page + t - j)
    new = np.where((j <= t) & (j < t_new), _bucket_np(t - j), MASK_CODE)
    return np.stack([last, new]).astype(np.int32)


def _rotation_tables(pos):
    half = DK_R // 2
    inv = 1.0 / (10000.0 ** jnp.linspace(0.0, 1.0, half, dtype=F32))
    ang = pos.astype(F32)[:, None] * inv[None, :]
    cos = jnp.cos(ang)
    sin = jnp.sin(ang)
    cos_t = jnp.tile(cos, (1, LANES // half))
    sin_t = jnp.tile(jnp.concatenate([-sin, sin], axis=1), (1, LANES // DK_R))
    return cos_t, sin_t


def _log_gamma_np():
    return np.log(1.0 - 2.0 ** (-5.0 - np.arange(H_R, dtype=np.float64)))


def _table_kernel(vals_ref, code_ref, out_ref):
    h = pl.program_id(1)
    code = code_ref[0]
    acc = jnp.full(code.shape, vals_ref[MASK_CODE, h], F32)
    for b in range(NUM_BUCKETS):
        acc = jnp.where(code == b, vals_ref[b, h], acc)
    out_ref[0, 0] = acc


def _bias_tables(vals, codes):
    n, r, c = codes.shape
    return pl.pallas_call(
        _table_kernel,
        grid=(n, H_D),
        in_specs=[pl.BlockSpec(memory_space=pltpu.SMEM),
                  pl.BlockSpec((1, r, c), lambda t, h: (t, 0, 0))],
        out_specs=pl.BlockSpec((1, 1, r, c), lambda t, h: (t, h, 0, 0)),
        out_shape=jax.ShapeDtypeStruct((n, H_D, r, c), F32),
        compiler_params=_cparams(2),
        name="bias_tables",
    )(vals, codes)


def _inproj_kernel(x_ref, g_ref, w_ref, cos_ref, sin_ref,
                   k_ref, v_ref, q_ref, gd_ref, qr_ref, kr_ref, vr_ref, gr_ref):
    x = x_ref[...]
    inv = lax.rsqrt(jnp.mean(x * x, axis=-1, keepdims=True) + EPS)
    h = (x * inv * g_ref[...]).astype(BF16)

    def proj(i):
        return jnp.dot(h, w_ref[:, _OFF[i]:_OFF[i + 1]], preferred_element_type=F32)

    q_ref[...] = (proj(0) * Q_PRESCALE).astype(BF16)
    k_ref[...] = proj(1)
    v_ref[...] = proj(2)
    gd_ref[...] = proj(3).astype(BF16)

    cos = cos_ref[...]
    sin = sin_ref[...]
    lane = lax.broadcasted_iota(jnp.int32, cos.shape, 1)
    low_half = (lane % DK_R) < (DK_R // 2)

    def rotate(z):
        outs = []
        for c in range(z.shape[1] // LANES):
            zc = z[:, c * LANES:(c + 1) * LANES]
            partner = jnp.where(low_half,
                                pltpu.roll(zc, LANES - DK_R // 2, 1),
                                pltpu.roll(zc, DK_R // 2, 1))
            outs.append(zc * cos + partner * sin)
        return jnp.concatenate(outs, axis=1)

    qr_ref[...] = rotate(proj(4)).astype(BF16)
    kr_ref[...] = (rotate(proj(5)) * (DK_R ** -0.5)).astype(BF16)
    vr_ref[...] = proj(6).astype(BF16)
    gr_ref[...] = proj(7).astype(BF16)


def _in_projection(x2d, g_pre, w_bf16, cos_t, sin_t, tm):
    n, d = x2d.shape
    n_tab = cos_t.shape[0] // tm
    row = lambda i: (i, 0)
    tab = lambda i: (i % n_tab, 0)
    widths = [(W_D, F32), (W_D, F32), (2 * H_D * D_QK, BF16), (W_D, BF16),
              (H_R * DK_R, BF16), (H_R * DK_R, BF16), (W_R, BF16), (W_R, BF16)]
    return pl.pallas_call(
        _inproj_kernel,
        grid=(n // tm,),
        in_specs=[pl.BlockSpec((tm, d), row),
                  pl.BlockSpec((1, d), lambda i: (0, 0)),
                  pl.BlockSpec(w_bf16.shape, lambda i: (0, 0)),
                  pl.BlockSpec((tm, LANES), tab),
                  pl.BlockSpec((tm, LANES), tab)],
        out_specs=[pl.BlockSpec((tm, w), row) for w, _ in widths],
        out_shape=[jax.ShapeDtypeStruct((n, w), dt) for w, dt in widths],
        compiler_params=_cparams(1),
        name="in_projection",
    )(x2d, g_pre, w_bf16, cos_t, sin_t)


def _attn_kernel(lam_ref, q_ref, k_ref, v_ref, tab_ref, o_ref, kb, vb, m_sc, acc_sc, *, t):
    i = pl.program_id(2)
    s_len = k_ref.shape[0]

    @pl.when(i == 0)
    def _():
        kb[...] = k_ref[...].astype(BF16)
        vb[:, :DV_D] = v_ref[...].astype(BF16)
        vb[:, DV_D:] = jnp.ones((s_len, DV_D), BF16)

    q = q_ref[...]
    lane = lax.broadcasted_iota(jnp.int32, q.shape, 1)
    zero = jnp.zeros_like(q)
    qs = jnp.concatenate([jnp.where(lane < D_QK, q, zero), jnp.where(lane >= D_QK, q, zero)], axis=0)
    m_sc[...] = jnp.full(m_sc.shape, NEG, F32)
    acc_sc[...] = jnp.zeros(acc_sc.shape, F32)

    def tile(j, tab):
        start = pl.multiple_of(j * t, t)
        s = lax.dot_general(qs, kb[pl.ds(start, t), :], (((1,), (1,)), ((), ())),
                            preferred_element_type=F32)
        if tab is not None:
            s = s + jnp.concatenate([tab, tab], axis=0)
        m_prev = m_sc[...]
        m_new = jnp.maximum(m_prev, jnp.max(s, axis=1, keepdims=True))
        corr = jnp.exp2(m_prev - m_new)
        p = jnp.exp2(s - jnp.concatenate([m_new] * (t // LANES), axis=1))
        pv = jnp.dot(p.astype(BF16), vb[pl.ds(start, t), :], preferred_element_type=F32)
        acc_sc[...] = acc_sc[...] * jnp.concatenate([corr, corr], axis=1) + pv
        m_sc[...] = m_new

    lax.fori_loop(0, i - 1, lambda j, c: (tile(j, None), c)[1], 0)

    @pl.when(i >= 1)
    def _():
        tile(i - 1, tab_ref[1, 0])

    tile(i, tab_ref[0, 0])

    acc = acc_sc[...]
    n = acc[:, :DV_D] / acc[:, DV_D:]
    o_ref[...] = (n[:t] - lam_ref[0] * n[t:]).astype(o_ref.dtype)


def _prompt_attention(lam, q2d, k2d, v2d, tabs, batch, seq, t):
    nq = seq // t
    qmap = lambda b, h, i: (b * nq + i, h)
    kvmap = lambda b, h, i: (b, h)
    return pl.pallas_call(
        functools.partial(_attn_kernel, t=t),
        grid=(batch, H_D, nq),
        in_specs=[pl.BlockSpec(memory_space=pltpu.SMEM),
                  pl.BlockSpec((t, 2 * D_QK), qmap),
                  pl.BlockSpec((seq, 2 * D_QK), kvmap),
                  pl.BlockSpec((seq, DV_D), kvmap),
                  pl.BlockSpec((2, 1, t, t), lambda b, h, i: (0, h, 0, 0))],
        out_specs=pl.BlockSpec((t, DV_D), qmap),
        out_shape=jax.ShapeDtypeStruct((batch * seq, W_D), BF16),
        scratch_shapes=[pltpu.VMEM((seq, 2 * D_QK), BF16),
                        pltpu.VMEM((seq, 2 * DV_D), BF16),
                        pltpu.VMEM((2 * t, LANES), F32),
                        pltpu.VMEM((2 * t, 2 * DV_D), F32)],
        compiler_params=_cparams(3),
        name="prompt_attention",
    )(lam, q2d, k2d, v2d, tabs)


def _ret_kernel(qr_ref, kr_ref, vr_ref, dec_ref, qdec_ref, kdec_ref, o_ref, s_out_ref, st_sc, *, g_chunk):
    c = pl.program_id(1)

    @pl.when(c == 0)
    def _():
        st_sc[...] = jnp.zeros(st_sc.shape, F32)

    q = qr_ref[...]
    k = kr_ref[...]
    v = vr_ref[...]
    head_of_lane = lax.broadcasted_iota(jnp.int32, q.shape, 1) // DK_R
    st = st_sc[...]
    st_b = st.astype(BF16)
    kd_t = (k.astype(F32) * kdec_ref[...]).T
    for h in range(H_R):
        qh = jnp.where(head_of_lane == h, q, jnp.zeros_like(q))
        vh = v[:, h * DV_R:(h + 1) * DV_R]
        sc = lax.dot_general(qh, k, (((1,), (1,)), ((), ())), preferred_element_type=F32) * dec_ref[h]
        inner = jnp.dot(sc.astype(BF16), vh, preferred_element_type=F32)
        cross = jnp.dot(qh, st_b, preferred_element_type=F32) * qdec_ref[h]
        o_ref[:, h * DV_R:(h + 1) * DV_R] = (inner + cross).astype(o_ref.dtype)
        rows = slice(h * DK_R, (h + 1) * DK_R)
        upd = jnp.dot(kd_t[rows, :].astype(BF16), vh, preferred_element_type=F32)
        st_new = g_chunk[h] * st[rows, :] + upd
        st_sc[rows, :] = st_new
        s_out_ref[0, h] = st_new


def _prompt_retention(qr, kr, vr, batch, seq):
    c = min(RET_CHUNK, seq)
    nc = seq // c
    lg = _log_gamma_np()
    idx = np.arange(c, dtype=np.float64)
    d = idx[:, None] - idx[None, :]
    decay = np.where(d[None] >= 0, np.exp(lg[:, None, None] * np.maximum(d, 0.0)[None]), 0.0)
    qdec = np.broadcast_to(np.exp(lg[:, None] * (idx + 1.0)[None, :])[:, :, None], (H_R, c, DV_R))
    kdec = np.repeat(np.exp(lg[:, None] * (c - 1.0 - idx)[None, :]).T, DK_R, axis=1)
    g_chunk = tuple(float(v) for v in np.exp(lg * c))
    row = lambda b, j: (b * nc + j, 0)
    full3 = lambda b, j: (0, 0, 0)
    return pl.pallas_call(
        functools.partial(_ret_kernel, g_chunk=g_chunk),
        grid=(batch, nc),
        in_specs=[pl.BlockSpec((c, H_R * DK_R), row),
                  pl.BlockSpec((c, H_R * DK_R), row),
                  pl.BlockSpec((c, W_R), row),
                  pl.BlockSpec((H_R, c, c), full3),
                  pl.BlockSpec((H_R, c, DV_R), full3),
                  pl.BlockSpec((c, H_R * DK_R), lambda b, j: (0, 0))],
        out_specs=[pl.BlockSpec((c, W_R), row),
                   pl.BlockSpec((1, H_R, DK_R, DV_R), lambda b, j: (b, 0, 0, 0))],
        out_shape=[jax.ShapeDtypeStruct((batch * seq, W_R), BF16),
                   jax.ShapeDtypeStruct((batch, H_R, DK_R, DV_R), F32)],
        scratch_shapes=[pltpu.VMEM((H_R * DK_R, DV_R), F32)],
        compiler_params=_cparams(2),
        name="prompt_retention",
    )(qr, kr, vr, jnp.asarray(decay, F32), jnp.asarray(qdec, F32), jnp.asarray(kdec, F32))


def _decode_kernel(pt_ref, lam_ref, q_ref, kn_ref, vn_ref, tab_ref, ck_hbm, cv_hbm, o_ref,
                   kbuf, vbuf, knew, vnew, m_sc, l_sc, acc_sc, sem, *, n_batch, n_pages, group, n_slot, t_new):
    n_group = n_pages // group
    rows = 2 * t_new

    def page_copies(b, g, slot, real):
        out = []
        for p in range(group):
            phys = pt_ref[b * n_pages + g * group + p] if real else 0
            out.append(pltpu.make_async_copy(ck_hbm.at[phys], kbuf.at[slot, p], sem.at[0, slot]))
            out.append(pltpu.make_async_copy(cv_hbm.at[phys], vbuf.at[slot, p], sem.at[1, slot]))
        return out

    def start_group(b, g, slot):
        for cp in page_copies(b, g, slot, True):
            cp.start()

    b = pl.program_id(0)
    g = pl.program_id(1)

    @pl.when((b == 0) & (g == 0))
    def _():
        for g0 in range(min(n_slot, n_group)):
            start_group(0, g0, g0 % n_slot)
        knew[...] = jnp.zeros(knew.shape, F32)
        vnew[...] = jnp.zeros(vnew.shape, F32)

    row = lax.broadcasted_iota(jnp.int32, (rows, 2 * D_QK), 0)
    lane = lax.broadcasted_iota(jnp.int32, (rows, 2 * D_QK), 1)
    own_half = (row < t_new) == (lane < D_QK)

    def softmax_update(state, scores, values):
        m, l, acc = state
        mx = scores[0]
        for s in scores[1:]:
            mx = jnp.maximum(mx, s)
        m_new = jnp.maximum(m, jnp.broadcast_to(jnp.max(mx, axis=1, keepdims=True), m.shape))
        corr = jnp.exp2(m - m_new)
        ps = [jnp.exp2(s - m_new) for s in scores]
        psum = ps[0]
        for p in ps[1:]:
            psum = psum + p
        l_new = l * corr + jnp.broadcast_to(jnp.sum(psum, axis=1, keepdims=True), l.shape)
        pv = jnp.dot(ps[0].astype(BF16), values[0], preferred_element_type=F32)
        for p, vv in zip(ps[1:], values[1:]):
            pv = pv + jnp.dot(p.astype(BF16), vv, preferred_element_type=F32)
        return m_new, l_new, acc * corr + pv

    def qk(q8, keys):
        return lax.dot_general(q8, keys, (((1,), (1,)), ((), ())), preferred_element_type=F32)

    qb = q_ref[0]
    q8 = [jnp.where(own_half, qb[:, h * 2 * D_QK:(h + 1) * 2 * D_QK], 0.0).astype(BF16)
          for h in range(H_D)]

    @pl.when(g == 0)
    def _():
        m_sc[...] = jnp.full(m_sc.shape, NEG, F32)
        l_sc[...] = jnp.zeros(l_sc.shape, F32)
        acc_sc[...] = jnp.zeros(acc_sc.shape, F32)

    slot = lax.rem(g, n_slot)
    for cp in page_copies(b, g, slot, False):
        cp.wait()
    last = jnp.where(g == n_group - 1, 1.0, 0.0).astype(F32)
    for h in range(H_D):
        cols = slice(h * DV_D, (h + 1) * DV_D)
        scores = [qk(q8[h], kbuf[slot, p, :, cols].astype(BF16)) for p in range(group)]
        scores[-1] = scores[-1] + tab_ref[0, h] * last
        values = [vbuf[slot, p, :, cols].astype(BF16) for p in range(group)]
        m_sc[h], l_sc[h], acc_sc[h] = softmax_update((m_sc[h], l_sc[h], acc_sc[h]), scores, values)

    nxt = b * n_group + g + n_slot

    @pl.when(nxt < n_batch * n_group)
    def _():
        start_group(nxt // n_group, lax.rem(nxt, n_group), slot)

    @pl.when(g == n_group - 1)
    def _():
        knew[0:rows, :] = kn_ref[0]
        vnew[0:rows, :] = vn_ref[0]
        for h in range(H_D):
            cols = slice(h * DV_D, (h + 1) * DV_D)
            s_new = qk(q8[h], knew[:, cols].astype(BF16)) + tab_ref[1, h]
            _, l, acc = softmax_update((m_sc[h], l_sc[h], acc_sc[h]), [s_new], [vnew[:, cols].astype(BF16)])
            n = acc / l
            o_ref[0, :, cols] = n - lam_ref[0] * pltpu.roll(n, t_new, 0)


def _decode_attention(page_table, lam, q_s, k_new, v_new, tabs, cache_k, cache_v, group=8, n_slot=4):
    n_batch, rows, _ = q_s.shape
    n_pages = page_table.shape[1]
    page = cache_k.shape[1]
    group = math.gcd(group, n_pages)
    n_slot = max(1, math.gcd(n_slot, n_pages // group))
    per_b = pl.BlockSpec((1, rows, W_D), lambda b, g: (b, 0, 0))
    return pl.pallas_call(
        functools.partial(_decode_kernel, n_batch=n_batch, n_pages=n_pages, group=group,
                          n_slot=n_slot, t_new=rows // 2),
        grid=(n_batch, n_pages // group),
        in_specs=[pl.BlockSpec(memory_space=pltpu.SMEM),
                  pl.BlockSpec(memory_space=pltpu.SMEM),
                  per_b, per_b, per_b,
                  pl.BlockSpec(tabs.shape, lambda b, g: (0, 0, 0, 0)),
                  pl.BlockSpec(memory_space=pl.ANY),
                  pl.BlockSpec(memory_space=pl.ANY)],
        out_specs=per_b,
        out_shape=jax.ShapeDtypeStruct(q_s.shape, F32),
        scratch_shapes=[pltpu.VMEM((n_slot, group, page, W_D), F32),
                        pltpu.VMEM((n_slot, group, page, W_D), F32),
                        pltpu.VMEM((page, W_D), F32),
                        pltpu.VMEM((page, W_D), F32),
                        pltpu.VMEM((H_D, rows, LANES), F32),
                        pltpu.VMEM((H_D, rows, LANES), F32),
                        pltpu.VMEM((H_D, rows, DV_D), F32),
                        pltpu.SemaphoreType.DMA((2, n_slot))],
        compiler_params=_cparams(2),
        name="decode_attention",
    )(page_table.reshape(-1), lam, q_s, k_new, v_new, tabs, cache_k, cache_v)


def _sample_ret_kernel(qt_ref, kt_ref, v_ref, s_ref, o_ref, s_out_ref, *, gamma, bb, t_new):
    for i in range(bb):
        v = v_ref[i]
        for h in range(H_R):
            st = s_ref[i, h]
            qt = qt_ref[i, h]
            kt = kt_ref[i, h]
            outs = []
            for t in range(t_new):
                vrow = v[t:t + 1, h * DV_R:(h + 1) * DV_R]
                st = gamma[h] * st + kt[:, t:t + 1] * vrow
                outs.append(jnp.sum(qt[:, t:t + 1] * st, axis=0, keepdims=True))
            o_ref[i, :, h * DV_R:(h + 1) * DV_R] = jnp.concatenate(outs, axis=0)
            s_out_ref[i, h] = st


def _sample_retention(qr_t, kr_t, vr, state, bb=8):
    n_batch, _, _, t_new = qr_t.shape
    bb = math.gcd(bb, n_batch)
    gamma = tuple(float(v) for v in np.exp(_log_gamma_np()))
    m4 = lambda i: (i, 0, 0, 0)
    m3 = lambda i: (i, 0, 0)
    return pl.pallas_call(
        functools.partial(_sample_ret_kernel, gamma=gamma, bb=bb, t_new=t_new),
        grid=(n_batch // bb,),
        in_specs=[pl.BlockSpec((bb, H_R, DK_R, t_new), m4),
                  pl.BlockSpec((bb, H_R, DK_R, t_new), m4),
                  pl.BlockSpec((bb, t_new, W_R), m3),
                  pl.BlockSpec((bb, H_R, DK_R, DV_R), m4)],
        out_specs=[pl.BlockSpec((bb, t_new, W_R), m3),
                   pl.BlockSpec((bb, H_R, DK_R, DV_R), m4)],
        out_shape=[jax.ShapeDtypeStruct((n_batch, t_new, W_R), F32),
                   jax.ShapeDtypeStruct(state.shape, F32)],
        compiler_params=_cparams(1),
        name="sample_retention",
    )(qr_t, kr_t, vr, state)


def _merge_kernel(x_ref, od_ref, or_ref, gd_ref, gr_ref, gnd_ref, gnr_ref, w_ref, gpost_ref, y_ref):
    def gated(o_ref_, g_ref_, gn):
        parts = []
        for h in range(o_ref_.shape[1] // LANES):
            cols = slice(h * LANES, (h + 1) * LANES)
            o = o_ref_[:, cols].astype(F32)
            g = g_ref_[:, cols].astype(F32)
            y = o * lax.rsqrt(jnp.mean(o * o, axis=-1, keepdims=True) + EPS) * gn
            parts.append((y * (g * jax.nn.sigmoid(g))).astype(BF16))
        return parts

    o = jnp.concatenate(gated(od_ref, gd_ref, gnd_ref[...]) + gated(or_ref, gr_ref, gnr_ref[...]), axis=1)
    mix = jnp.dot(o, w_ref[...], preferred_element_type=F32)
    y = mix * lax.rsqrt(jnp.mean(mix * mix, axis=-1, keepdims=True) + EPS) * gpost_ref[...]
    y_ref[...] = x_ref[...] + y


def _merge(x2d, od, orr, gd, gr, gn_d, gn_r, w_bf16, g_post, tm):
    n, d = x2d.shape
    row = lambda i: (i, 0)
    fix = lambda i: (0, 0)
    return pl.pallas_call(
        _merge_kernel,
        grid=(n // tm,),
        in_specs=[pl.BlockSpec((tm, d), row),
                  pl.BlockSpec((tm, W_D), row),
                  pl.BlockSpec((tm, W_R), row),
                  pl.BlockSpec((tm, W_D), row),
                  pl.BlockSpec((tm, W_R), row),
                  pl.BlockSpec((1, DV_D), fix),
                  pl.BlockSpec((1, DV_R), fix),
                  pl.BlockSpec(w_bf16.shape, fix),
                  pl.BlockSpec((1, d), fix)],
        out_specs=pl.BlockSpec((tm, d), row),
        out_shape=jax.ShapeDtypeStruct((n, d), F32),
        compiler_params=_cparams(1),
        name="merge_out_projection",
    )(x2d, od, orr, gd, gr, gn_d, gn_r, w_bf16, g_post)


def _pick_tile(n, target):
    t = math.gcd(n, target)
    assert t % 8 == 0, (n, target)
    return t


def kernel(x_prompt, x_sample, cache_k, cache_v, state_ret, page_table, rel_bias, norm_pre, norm_post,
           w_in, w_out, lambda_q1, lambda_k1, lambda_q2, lambda_k2, gn_diff, gn_ret):
    depth = w_in.shape[0]
    batch, seq, d_model = x_prompt.shape
    dec_batch, t_new, _ = x_sample.shape
    n_pool, page = cache_k.shape[1], cache_k.shape[2]
    n_pages = page_table.shape[1]
    past = n_pages * page
    far = _far_distance()

    t_attn = _pick_tile(seq, 512)
    assert t_attn > far and page > far, "bias must be constant beyond the neighbouring tile / page"
    tm_p = _pick_tile(batch * seq, 512)
    assert seq % tm_p == 0
    tm_s = _pick_tile(dec_batch * t_new, 512)

    vals = jnp.concatenate([(rel_bias - rel_bias[NUM_BUCKETS - 1:]) * LOG2E,
                            jnp.full((1, H_D), NEG, F32)], axis=0).astype(F32)
    tabs_p = _bias_tables(vals, jnp.asarray(_prompt_bucket_codes(t_attn)))
    tabs_s = _bias_tables(vals, jnp.asarray(_decode_bucket_codes(page, t_new)))

    cos_p, sin_p = _rotation_tables(jnp.arange(seq, dtype=jnp.int32))
    pos_s = past + (jnp.arange(tm_s, dtype=jnp.int32) % t_new)
    cos_s, sin_s = _rotation_tables(pos_s)

    yp = x_prompt.reshape(batch * seq, d_model)
    ys = x_sample.reshape(dec_batch * t_new, d_model)
    kp_l, vp_l, sp_l, ks_l, vs_l, ss_l = [], [], [], [], [], []
    for layer in range(depth):
        lambda_init = 0.8 - 0.6 * math.exp(-0.3 * layer)
        lam = (jnp.exp(jnp.sum(lambda_q1[layer].astype(F32) * lambda_k1[layer].astype(F32)))
               - jnp.exp(jnp.sum(lambda_q2[layer].astype(F32) * lambda_k2[layer].astype(F32)))
               + lambda_init).reshape(1).astype(F32)
        w_in_b = w_in[layer].astype(BF16)
        w_out_b = w_out[layer].astype(BF16)
        g_pre = norm_pre[layer].reshape(1, d_model)
        g_post = norm_post[layer].reshape(1, d_model)
        gn_d = (gn_diff[layer].astype(F32) * (1.0 - lambda_init)).reshape(1, DV_D)
        gn_r = gn_ret[layer].astype(F32).reshape(1, DV_R)

        k_p, v_p, q_p, gd_p, qr_p, kr_p, vr_p, gr_p = _in_projection(yp, g_pre, w_in_b, cos_p, sin_p, tm_p)
        od_p = _prompt_attention(lam, q_p, k_p, v_p, tabs_p, batch, seq, t_attn)
        or_p, s_p = _prompt_retention(qr_p, kr_p, vr_p, batch, seq)
        yp = _merge(yp, od_p, or_p, gd_p, gr_p, gn_d, gn_r, w_out_b, g_post, tm_p)
        kp_l.append(k_p.reshape(batch, seq, H_D, 2 * D_QK))
        vp_l.append(v_p.reshape(batch, seq, H_D, DV_D))
        sp_l.append(s_p)

        k_s, v_s, q_s, gd_s, qr_s, kr_s, vr_s, gr_s = _in_projection(ys, g_pre, w_in_b, cos_s, sin_s, tm_s)
        q3 = q_s.astype(F32).reshape(dec_batch, t_new, W_D)
        pad = lambda a: jnp.pad(a.reshape(dec_batch, t_new, W_D), ((0, 0), (0, t_new), (0, 0)))
        od_s = _decode_attention(page_table + layer * n_pool, lam, jnp.concatenate([q3, q3], axis=1),
                                 pad(k_s), pad(v_s), tabs_s,
                                 cache_k.reshape(depth * n_pool, page, W_D),
                                 cache_v.reshape(depth * n_pool, page, W_D))
        od_s = od_s[:, :t_new].reshape(dec_batch * t_new, W_D)
        to_cols = lambda a: a.astype(F32).reshape(dec_batch, t_new, H_R, DK_R).transpose(0, 2, 3, 1)
        or_s, s_s = _sample_retention(to_cols(qr_s), to_cols(kr_s),
                                      vr_s.astype(F32).reshape(dec_batch, t_new, W_R),
                                      state_ret[layer].astype(F32))
        ys = _merge(ys, od_s, or_s.reshape(dec_batch * t_new, W_R), gd_s, gr_s, gn_d, gn_r, w_out_b, g_post, tm_s)
        ks_l.append(k_s.reshape(dec_batch, t_new, H_D, 2 * D_QK))
        vs_l.append(v_s.reshape(dec_batch, t_new, H_D, DV_D))
        ss_l.append(s_s)

    stack = lambda parts: parts[0][None] if len(parts) == 1 else jnp.stack(parts)
    return (yp.reshape(batch, seq, d_model), ys.reshape(dec_batch, t_new, d_model),
            stack(kp_l), stack(vp_l), stack(sp_l), stack(ks_l), stack(vs_l), stack(ss_l))
```

```python
import functools
import math

import jax
import jax.numpy as jnp
import numpy as np
from jax import lax
from jax.experimental import pallas as pl
from jax.experimental.pallas import tpu as pltpu

F32 = jnp.float32
BF16 = jnp.bfloat16

H_D = 4
D_QK = 64
DV_D = 128
W_D = H_D * DV_D
H_R = 4
DK_R = 64
DV_R = 128
W_R = H_R * DV_R
NUM_BUCKETS = 32
MAX_DISTANCE = 128
RET_CHUNK = 128
EPS = 1e-6
NEG = -1e30
LOG2E = 1.4426950408889634
Q_PRESCALE = D_QK ** -0.5 * LOG2E
MASK_CODE = NUM_BUCKETS
LANES = 128
VMEM_LIMIT = 56 * 1024 * 1024

_SIZES = [H_D * 2 * D_QK, H_D * 2 * D_QK, W_D, W_D, H_R * DK_R, H_R * DK_R, W_R, W_R]
_OFF = [0] + [int(v) for v in np.cumsum(_SIZES)]


def _cparams(n_axes):
    return pltpu.CompilerParams(dimension_semantics=("arbitrary",) * n_axes,
                                vmem_limit_bytes=VMEM_LIMIT)


def _bucket_np(rel):
    n = np.maximum(rel, 0)
    max_exact = NUM_BUCKETS // 2
    nf = np.maximum(n, max_exact).astype(np.float32)
    large = max_exact + (np.log(nf / np.float32(max_exact)) / np.float32(math.log(MAX_DISTANCE / max_exact))
                         * np.float32(NUM_BUCKETS - max_exact)).astype(np.int32)
    large = np.minimum(large, NUM_BUCKETS - 1)
    return np.where(n < max_exact, n, large).astype(np.int32)


def _far_distance():
    b = _bucket_np(np.arange(4 * MAX_DISTANCE))
    return int(np.max(np.nonzero(b != NUM_BUCKETS - 1)[0])) + 1


def _prompt_bucket_codes(t):
    r = np.arange(t)[:, None]
    c = np.arange(t)[None, :]
    diag = np.where(r >= c, _bucket_np(r - c), MASK_CODE)
    left = _bucket_np(r - c + t)
    return np.stack([diag, left]).astype(np.int32)[:, None]


def _decode_bucket_codes(page, t_new):
    h = np.arange(H_D)[:, None, None]
    t = np.tile(np.arange(t_new), 2)[None, :, None]
    col = np.arange(page * H_D)[None, None, :]
    j, hk = col // H_D, col % H_D
    own = hk == h
    far = np.where(own, NUM_BUCKETS - 1, MASK_CODE) + 0 * t
    last = np.where(own, _bucket_np(page + t - j), MASK_CODE)
    new = np.where(own & (j <= t) & (j < t_new), _bucket_np(t - j), MASK_CODE)
    return np.stack([far, last, new]).astype(np.int32)


def _rotation_tables(pos):
    half = DK_R // 2
    inv = 1.0 / (10000.0 ** jnp.linspace(0.0, 1.0, half, dtype=F32))
    ang = pos.astype(F32)[:, None] * inv[None, :]
    cos = jnp.cos(ang)
    sin = jnp.sin(ang)
    cos_t = jnp.tile(cos, (1, LANES // half))
    sin_t = jnp.tile(jnp.concatenate([-sin, sin], axis=1), (1, LANES // DK_R))
    return cos_t, sin_t


def _log_gamma_np():
    return np.log(1.0 - 2.0 ** (-5.0 - np.arange(H_R, dtype=np.float64)))


def _table_kernel(vals_ref, code_ref, out_ref):
    h = pl.program_id(1)
    code = code_ref[0, 0]
    acc = jnp.full(code.shape, vals_ref[MASK_CODE, h], F32)
    for b in range(NUM_BUCKETS):
        acc = jnp.where(code == b, vals_ref[b, h], acc)
    out_ref[0, 0] = acc


def _bias_tables(vals, codes):
    n, hc, r, c = codes.shape
    code_map = (lambda t, h: (t, h, 0, 0)) if hc == H_D else (lambda t, h: (t, 0, 0, 0))
    return pl.pallas_call(
        _table_kernel,
        grid=(n, H_D),
        in_specs=[pl.BlockSpec(memory_space=pltpu.SMEM),
                  pl.BlockSpec((1, 1, r, c), code_map)],
        out_specs=pl.BlockSpec((1, 1, r, c), lambda t, h: (t, h, 0, 0)),
        out_shape=jax.ShapeDtypeStruct((n, H_D, r, c), F32),
        compiler_params=_cparams(2),
        name="bias_tables",
    )(vals, codes)


def _inproj_kernel(x_ref, g_ref, w_ref, cos_ref, sin_ref,
                   k_ref, v_ref, kb_ref, vb_ref, q_ref, gd_ref, qr_ref, kr_ref, vr_ref, gr_ref):
    tm = x_ref.shape[0]
    x = x_ref[...]
    inv = lax.rsqrt(jnp.mean(x * x, axis=-1, keepdims=True) + EPS)
    h = (x * inv * g_ref[...]).astype(BF16)

    def proj(i):
        return jnp.dot(h, w_ref[:, _OFF[i]:_OFF[i + 1]], preferred_element_type=F32)

    def store_heads_as_rows(ref, z):
        for hd in range(H_D):
            ref[pl.ds(hd, tm, stride=H_D), :] = z[:, hd * LANES:(hd + 1) * LANES]

    q_ref[...] = (proj(0) * Q_PRESCALE).astype(BF16)
    zk = proj(1)
    store_heads_as_rows(k_ref, zk)
    kb_ref[...] = zk.astype(BF16)
    zv = proj(2)
    store_heads_as_rows(v_ref, zv)
    vb_ref[...] = zv.astype(BF16)
    gd_ref[...] = proj(3).astype(BF16)

    cos = cos_ref[...]
    sin = sin_ref[...]
    lane = lax.broadcasted_iota(jnp.int32, cos.shape, 1)
    low_half = (lane % DK_R) < (DK_R // 2)

    def rotate(z):
        outs = []
        for c in range(z.shape[1] // LANES):
            zc = z[:, c * LANES:(c + 1) * LANES]
            partner = jnp.where(low_half,
                                pltpu.roll(zc, LANES - DK_R // 2, 1),
                                pltpu.roll(zc, DK_R // 2, 1))
            outs.append(zc * cos + partner * sin)
        return jnp.concatenate(outs, axis=1)

    qr_ref[...] = rotate(proj(4)).astype(BF16)
    kr_ref[...] = (rotate(proj(5)) * (DK_R ** -0.5)).astype(BF16)
    vr_ref[...] = proj(6).astype(BF16)
    gr_ref[...] = proj(7).astype(BF16)


def _in_projection(x2d, g_pre, w_bf16, cos_t, sin_t, tm):
    n, d = x2d.shape
    n_tab = cos_t.shape[0] // tm
    row = lambda i: (i, 0)
    tab = lambda i: (i % n_tab, 0)
    outs = [(H_D, 2 * D_QK, F32), (H_D, DV_D, F32), (1, W_D, BF16), (1, W_D, BF16),
            (1, 2 * H_D * D_QK, BF16), (1, W_D, BF16),
            (1, H_R * DK_R, BF16), (1, H_R * DK_R, BF16), (1, W_R, BF16), (1, W_R, BF16)]
    return pl.pallas_call(
        _inproj_kernel,
        grid=(n // tm,),
        in_specs=[pl.BlockSpec((tm, d), row),
                  pl.BlockSpec((1, d), lambda i: (0, 0)),
                  pl.BlockSpec(w_bf16.shape, lambda i: (0, 0)),
                  pl.BlockSpec((tm, LANES), tab),
                  pl.BlockSpec((tm, LANES), tab)],
        out_specs=[pl.BlockSpec((tm * r, w), row) for r, w, _ in outs],
        out_shape=[jax.ShapeDtypeStruct((n * r, w), dt) for r, w, dt in outs],
        compiler_params=_cparams(1),
        name="in_projection",
    )(x2d, g_pre, w_bf16, cos_t, sin_t)


def _attn_kernel(lam_ref, q_ref, kb, v_ref, tab_ref, o_ref, vb, m_sc, acc_sc, *, t):
    i = pl.program_id(2)
    s_len = kb.shape[0]

    @pl.when(i == 0)
    def _():
        vb[:, :DV_D] = v_ref[...]
        vb[:, DV_D:] = jnp.ones((s_len, DV_D), BF16)

    q = q_ref[...]
    lane = lax.broadcasted_iota(jnp.int32, q.shape, 1)
    zero = jnp.zeros_like(q)
    qs = jnp.concatenate([jnp.where(lane < D_QK, q, zero), jnp.where(lane >= D_QK, q, zero)], axis=0)
    m_sc[...] = jnp.full(m_sc.shape, NEG, F32)
    acc_sc[...] = jnp.zeros(acc_sc.shape, F32)

    def tile(j, tab):
        start = pl.multiple_of(j * t, t)
        s = lax.dot_general(qs, kb[pl.ds(start, t), :], (((1,), (1,)), ((), ())),
                            preferred_element_type=F32)
        if tab is not None:
            s = s + jnp.concatenate([tab, tab], axis=0)
        m_prev = m_sc[...]
        m_new = jnp.maximum(m_prev, jnp.max(s, axis=1, keepdims=True))
        corr = jnp.exp2(m_prev - m_new)
        p = jnp.exp2(s - jnp.concatenate([m_new] * (t // LANES), axis=1))
        pv = jnp.dot(p.astype(BF16), vb[pl.ds(start, t), :], preferred_element_type=F32)
        acc_sc[...] = acc_sc[...] * jnp.concatenate([corr, corr], axis=1) + pv
        m_sc[...] = m_new

    lax.fori_loop(0, i - 1, lambda j, c: (tile(j, None), c)[1], 0)

    @pl.when(i >= 1)
    def _():
        tile(i - 1, tab_ref[1, 0])

    tile(i, tab_ref[0, 0])

    acc = acc_sc[...]
    n = acc[:, :DV_D] / acc[:, DV_D:]
    o_ref[...] = (n[:t] - lam_ref[0] * n[t:]).astype(o_ref.dtype)


def _prompt_attention(lam, q2d, k2d, v2d, tabs, batch, seq, t):
    nq = seq // t
    qmap = lambda b, h, i: (b * nq + i, h)
    kvmap = lambda b, h, i: (b, h)
    return pl.pallas_call(
        functools.partial(_attn_kernel, t=t),
        grid=(batch, H_D, nq),
        in_specs=[pl.BlockSpec(memory_space=pltpu.SMEM),
                  pl.BlockSpec((t, 2 * D_QK), qmap),
                  pl.BlockSpec((seq, 2 * D_QK), kvmap),
                  pl.BlockSpec((seq, DV_D), kvmap),
                  pl.BlockSpec((2, 1, t, t), lambda b, h, i: (0, h, 0, 0))],
        out_specs=pl.BlockSpec((t, DV_D), qmap),
        out_shape=jax.ShapeDtypeStruct((batch * seq, W_D), BF16),
        scratch_shapes=[pltpu.VMEM((seq, 2 * DV_D), BF16),
                        pltpu.VMEM((2 * t, LANES), F32),
                        pltpu.VMEM((2 * t, 2 * DV_D), F32)],
        compiler_params=_cparams(3),
        name="prompt_attention",
    )(lam, q2d, k2d, v2d, tabs)


def _ret_kernel(qr_ref, kr_ref, vr_ref, dec_ref, qdec_ref, kdec_ref, o_ref, s_out_ref, st_sc, *, g_chunk):
    c = pl.program_id(1)

    @pl.when(c == 0)
    def _():
        st_sc[...] = jnp.zeros(st_sc.shape, F32)

    q = qr_ref[...]
    k = kr_ref[...]
    v = vr_ref[...]
    head_of_lane = lax.broadcasted_iota(jnp.int32, q.shape, 1) // DK_R
    st = st_sc[...]
    st_b = st.astype(BF16)
    kd_t = (k.astype(F32) * kdec_ref[...]).T
    for h in range(H_R):
        qh = jnp.where(head_of_lane == h, q, jnp.zeros_like(q))
        vh = v[:, h * DV_R:(h + 1) * DV_R]
        sc = lax.dot_general(qh, k, (((1,), (1,)), ((), ())), preferred_element_type=F32) * dec_ref[h]
        inner = jnp.dot(sc.astype(BF16), vh, preferred_element_type=F32)
        cross = jnp.dot(qh, st_b, preferred_element_type=F32) * qdec_ref[h]
        o_ref[:, h * DV_R:(h + 1) * DV_R] = (inner + cross).astype(o_ref.dtype)
        rows = slice(h * DK_R, (h + 1) * DK_R)
        upd = jnp.dot(kd_t[rows, :].astype(BF16), vh, preferred_element_type=F32)
        st_new = g_chunk[h] * st[rows, :] + upd
        st_sc[rows, :] = st_new
        s_out_ref[0, h] = st_new


def _prompt_retention(qr, kr, vr, batch, seq):
    c = min(RET_CHUNK, seq)
    nc = seq // c
    lg = _log_gamma_np()
    idx = np.arange(c, dtype=np.float64)
    d = idx[:, None] - idx[None, :]
    decay = np.where(d[None] >= 0, np.exp(lg[:, None, None] * np.maximum(d, 0.0)[None]), 0.0)
    qdec = np.broadcast_to(np.exp(lg[:, None] * (idx + 1.0)[None, :])[:, :, None], (H_R, c, DV_R))
    kdec = np.repeat(np.exp(lg[:, None] * (c - 1.0 - idx)[None, :]).T, DK_R, axis=1)
    g_chunk = tuple(float(v) for v in np.exp(lg * c))
    row = lambda b, j: (b * nc + j, 0)
    full3 = lambda b, j: (0, 0, 0)
    return pl.pallas_call(
        functools.partial(_ret_kernel, g_chunk=g_chunk),
        grid=(batch, nc),
        in_specs=[pl.BlockSpec((c, H_R * DK_R), row),
                  pl.BlockSpec((c, H_R * DK_R), row),
                  pl.BlockSpec((c, W_R), row),
                  pl.BlockSpec((H_R, c, c), full3),
                  pl.BlockSpec((H_R, c, DV_R), full3),
                  pl.BlockSpec((c, H_R * DK_R), lambda b, j: (0, 0))],
        out_specs=[pl.BlockSpec((c, W_R), row),
                   pl.BlockSpec((1, H_R, DK_R, DV_R), lambda b, j: (b, 0, 0, 0))],
        out_shape=[jax.ShapeDtypeStruct((batch * seq, W_R), BF16),
                   jax.ShapeDtypeStruct((batch, H_R, DK_R, DV_R), F32)],
        scratch_shapes=[pltpu.VMEM((H_R * DK_R, DV_R), F32)],
        compiler_params=_cparams(2),
        name="prompt_retention",
    )(qr, kr, vr, jnp.asarray(decay, F32), jnp.asarray(qdec, F32), jnp.asarray(kdec, F32))


def _decode_kernel(pt_ref, lam_ref, q_ref, kn_ref, vn_ref, tab_ref, ck_hbm, cv_hbm, o_ref,
                   kbuf, vbuf, knew, vnew, m_sc, l_sc, acc_sc, sem, *, n_batch, n_pages, group, n_slot, t_new):
    n_group = n_pages // group
    rows = H_D * 2 * t_new
    width = kbuf.shape[2]

    def page_copies(b, g, slot, real):
        out = []
        for p in range(group):
            phys = pt_ref[b * n_pages + g * group + p] if real else 0
            out.append(pltpu.make_async_copy(ck_hbm.at[phys], kbuf.at[slot, p], sem.at[0, slot]))
            out.append(pltpu.make_async_copy(cv_hbm.at[phys], vbuf.at[slot, p], sem.at[1, slot]))
        return out

    def start_group(b, g, slot):
        for cp in page_copies(b, g, slot, True):
            cp.start()

    b = pl.program_id(0)
    g = pl.program_id(1)

    @pl.when((b == 0) & (g == 0))
    def _():
        for g0 in range(min(n_slot, n_group)):
            start_group(0, g0, g0 % n_slot)
        knew[...] = jnp.zeros(knew.shape, F32)
        vnew[...] = jnp.zeros(vnew.shape, F32)

    row = lax.broadcasted_iota(jnp.int32, (rows, 2 * D_QK), 0)
    lane = lax.broadcasted_iota(jnp.int32, (rows, 2 * D_QK), 1)
    own_half = ((row // t_new) % 2 == 0) == (lane < D_QK)
    q = jnp.where(own_half, q_ref[0], 0.0).astype(BF16)

    def softmax_update(state, scores, values):
        m, l, acc = state
        mx = scores[0]
        for s in scores[1:]:
            mx = jnp.maximum(mx, s)
        m_new = jnp.maximum(m, jnp.broadcast_to(jnp.max(mx, axis=1, keepdims=True), m.shape))
        corr = jnp.exp2(m - m_new)
        m_wide = jnp.concatenate([m_new] * (width // LANES), axis=1)
        ps = [jnp.exp2(s - m_wide) for s in scores]
        psum = ps[0]
        for p in ps[1:]:
            psum = psum + p
        l_new = l * corr + jnp.broadcast_to(jnp.sum(psum, axis=1, keepdims=True), l.shape)
        pv = jnp.dot(ps[0].astype(BF16), values[0], preferred_element_type=F32)
        for p, vv in zip(ps[1:], values[1:]):
            pv = pv + jnp.dot(p.astype(BF16), vv, preferred_element_type=F32)
        return m_new, l_new, acc * corr + pv

    def qk(keys):
        return lax.dot_general(q, keys, (((1,), (1,)), ((), ())), preferred_element_type=F32)

    @pl.when(g == 0)
    def _():
        m_sc[...] = jnp.full(m_sc.shape, NEG, F32)
        l_sc[...] = jnp.zeros(l_sc.shape, F32)
        acc_sc[...] = jnp.zeros(acc_sc.shape, F32)

    slot = lax.rem(g, n_slot)
    for cp in page_copies(b, g, slot, False):
        cp.wait()
    scores = [qk(kbuf[slot, p].astype(BF16)) + tab_ref[0] for p in range(group - 1)]
    scores.append(qk(kbuf[slot, group - 1].astype(BF16)) + tab_ref[jnp.where(g == n_group - 1, 1, 0)])
    values = [vbuf[slot, p].astype(BF16) for p in range(group)]
    m_sc[...], l_sc[...], acc_sc[...] = softmax_update((m_sc[...], l_sc[...], acc_sc[...]), scores, values)

    nxt = b * n_group + g + n_slot

    @pl.when(nxt < n_batch * n_group)
    def _():
        start_group(nxt // n_group, lax.rem(nxt, n_group), slot)

    @pl.when(g == n_group - 1)
    def _():
        n_new = kn_ref.shape[1]
        knew[0:n_new, :] = kn_ref[0]
        vnew[0:n_new, :] = vn_ref[0]
        s_new = qk(knew[...].astype(BF16)) + tab_ref[2]
        _, l, acc = softmax_update((m_sc[...], l_sc[...], acc_sc[...]), [s_new], [vnew[...].astype(BF16)])
        n = acc / l
        o_ref[0] = n - lam_ref[0] * pltpu.roll(n, rows - t_new, 0)


def _decode_attention(page_table, lam, q_rows, k_new, v_new, tabs, cache_k, cache_v, group=8, n_slot=4):
    n_batch, rows, _ = q_rows.shape
    n_pages = page_table.shape[1]
    width = cache_k.shape[1]
    group = math.gcd(group, n_pages)
    n_slot = max(1, math.gcd(n_slot, n_pages // group))
    per_b = lambda a: pl.BlockSpec((1,) + a.shape[1:], lambda b, g: (b, 0, 0))
    return pl.pallas_call(
        functools.partial(_decode_kernel, n_batch=n_batch, n_pages=n_pages, group=group,
                          n_slot=n_slot, t_new=rows // (2 * H_D)),
        grid=(n_batch, n_pages // group),
        in_specs=[pl.BlockSpec(memory_space=pltpu.SMEM),
                  pl.BlockSpec(memory_space=pltpu.SMEM),
                  per_b(q_rows), per_b(k_new), per_b(v_new),
                  pl.BlockSpec(tabs.shape, lambda b, g: (0, 0, 0)),
                  pl.BlockSpec(memory_space=pl.ANY),
                  pl.BlockSpec(memory_space=pl.ANY)],
        out_specs=per_b(q_rows),
        out_shape=jax.ShapeDtypeStruct(q_rows.shape, F32),
        scratch_shapes=[pltpu.VMEM((n_slot, group, width, LANES), F32),
                        pltpu.VMEM((n_slot, group, width, LANES), F32),
                        pltpu.VMEM((width, LANES), F32),
                        pltpu.VMEM((width, LANES), F32),
                        pltpu.VMEM((rows, LANES), F32),
                        pltpu.VMEM((rows, LANES), F32),
                        pltpu.VMEM((rows, DV_D), F32),
                        pltpu.SemaphoreType.DMA((2, n_slot))],
        compiler_params=_cparams(2),
        name="decode_attention",
    )(page_table.reshape(-1), lam, q_rows, k_new, v_new, tabs, cache_k, cache_v)


def _sample_ret_kernel(qt_ref, kt_ref, v_ref, s_ref, o_ref, s_out_ref, *, gamma, bb, t_new):
    for i in range(bb):
        v = v_ref[i]
        for h in range(H_R):
            st = s_ref[i, h]
            qt = qt_ref[i, h]
            kt = kt_ref[i, h]
            outs = []
            for t in range(t_new):
                vrow = v[t:t + 1, h * DV_R:(h + 1) * DV_R]
                st = gamma[h] * st + kt[:, t:t + 1] * vrow
                outs.append(jnp.sum(qt[:, t:t + 1] * st, axis=0, keepdims=True))
            o_ref[i, :, h * DV_R:(h + 1) * DV_R] = jnp.concatenate(outs, axis=0)
            s_out_ref[i, h] = st


def _sample_retention(qr_t, kr_t, vr, state, bb=8):
    n_batch, _, _, t_new = qr_t.shape
    bb = math.gcd(bb, n_batch)
    gamma = tuple(float(v) for v in np.exp(_log_gamma_np()))
    m4 = lambda i: (i, 0, 0, 0)
    m3 = lambda i: (i, 0, 0)
    return pl.pallas_call(
        functools.partial(_sample_ret_kernel, gamma=gamma, bb=bb, t_new=t_new),
        grid=(n_batch // bb,),
        in_specs=[pl.BlockSpec((bb, H_R, DK_R, t_new), m4),
                  pl.BlockSpec((bb, H_R, DK_R, t_new), m4),
                  pl.BlockSpec((bb, t_new, W_R), m3),
                  pl.BlockSpec((bb, H_R, DK_R, DV_R), m4)],
        out_specs=[pl.BlockSpec((bb, t_new, W_R), m3),
                   pl.BlockSpec((bb, H_R, DK_R, DV_R), m4)],
        out_shape=[jax.ShapeDtypeStruct((n_batch, t_new, W_R), F32),
                   jax.ShapeDtypeStruct(state.shape, F32)],
        compiler_params=_cparams(1),
        name="sample_retention",
    )(qr_t, kr_t, vr, state)


def _merge_kernel(x_ref, od_ref, or_ref, gd_ref, gr_ref, gnd_ref, gnr_ref, w_ref, gpost_ref, y_ref):
    def gated(o_ref_, g_ref_, gn):
        parts = []
        for h in range(o_ref_.shape[1] // LANES):
            cols = slice(h * LANES, (h + 1) * LANES)
            o = o_ref_[:, cols].astype(F32)
            g = g_ref_[:, cols].astype(F32)
            y = o * lax.rsqrt(jnp.mean(o * o, axis=-1, keepdims=True) + EPS) * gn
            parts.append((y * (g * jax.nn.sigmoid(g))).astype(BF16))
        return parts

    o = jnp.concatenate(gated(od_ref, gd_ref, gnd_ref[...]) + gated(or_ref, gr_ref, gnr_ref[...]), axis=1)
    mix = jnp.dot(o, w_ref[...], preferred_element_type=F32)
    y = mix * lax.rsqrt(jnp.mean(mix * mix, axis=-1, keepdims=True) + EPS) * gpost_ref[...]
    y_ref[...] = x_ref[...] + y


def _merge(x2d, od, orr, gd, gr, gn_d, gn_r, w_bf16, g_post, tm):
    n, d = x2d.shape
    row = lambda i: (i, 0)
    fix = lambda i: (0, 0)
    return pl.pallas_call(
        _merge_kernel,
        grid=(n // tm,),
        in_specs=[pl.BlockSpec((tm, d), row),
                  pl.BlockSpec((tm, W_D), row),
                  pl.BlockSpec((tm, W_R), row),
                  pl.BlockSpec((tm, W_D), row),
                  pl.BlockSpec((tm, W_R), row),
                  pl.BlockSpec((1, DV_D), fix),
                  pl.BlockSpec((1, DV_R), fix),
                  pl.BlockSpec(w_bf16.shape, fix),
                  pl.BlockSpec((1, d), fix)],
        out_specs=pl.BlockSpec((tm, d), row),
        out_shape=jax.ShapeDtypeStruct((n, d), F32),
        compiler_params=_cparams(1),
        name="merge_out_projection",
    )(x2d, od, orr, gd, gr, gn_d, gn_r, w_bf16, g_post)


def _pick_tile(n, target):
    t = math.gcd(n, target)
    assert t % 8 == 0, (n, target)
    return t


def kernel(x_prompt, x_sample, cache_k, cache_v, state_ret, page_table, rel_bias, norm_pre, norm_post,
           w_in, w_out, lambda_q1, lambda_k1, lambda_q2, lambda_k2, gn_diff, gn_ret):
    depth = w_in.shape[0]
    batch, seq, d_model = x_prompt.shape
    dec_batch, t_new, _ = x_sample.shape
    n_pool, page = cache_k.shape[1], cache_k.shape[2]
    n_pages = page_table.shape[1]
    past = n_pages * page
    far = _far_distance()

    t_attn = _pick_tile(seq, 512)
    assert t_attn > far and page > far, "bias must be constant beyond the neighbouring tile / page"
    tm_p = _pick_tile(batch * seq, 512)
    assert seq % tm_p == 0
    tm_s = _pick_tile(dec_batch * t_new, 512)

    vals = jnp.concatenate([(rel_bias - rel_bias[NUM_BUCKETS - 1:]) * LOG2E,
                            jnp.full((1, H_D), NEG, F32)], axis=0).astype(F32)
    tabs_p = _bias_tables(vals, jnp.asarray(_prompt_bucket_codes(t_attn)))
    tabs_s = _bias_tables(vals, jnp.asarray(_decode_bucket_codes(page, t_new)))

    cos_p, sin_p = _rotation_tables(jnp.arange(seq, dtype=jnp.int32))
    pos_s = past + (jnp.arange(tm_s, dtype=jnp.int32) % t_new)
    cos_s, sin_s = _rotation_tables(pos_s)

    yp = x_prompt.reshape(batch * seq, d_model)
    ys = x_sample.reshape(dec_batch * t_new, d_model)
    kp_l, vp_l, sp_l, ks_l, vs_l, ss_l = [], [], [], [], [], []
    for layer in range(depth):
        lambda_init = 0.8 - 0.6 * math.exp(-0.3 * layer)
        lam = (jnp.exp(jnp.sum(lambda_q1[layer].astype(F32) * lambda_k1[layer].astype(F32)))
               - jnp.exp(jnp.sum(lambda_q2[layer].astype(F32) * lambda_k2[layer].astype(F32)))
               + lambda_init).reshape(1).astype(F32)
        w_in_b = w_in[layer].astype(BF16)
        w_out_b = w_out[layer].astype(BF16)
        g_pre = norm_pre[layer].reshape(1, d_model)
        g_post = norm_post[layer].reshape(1, d_model)
        gn_d = (gn_diff[layer].astype(F32) * (1.0 - lambda_init)).reshape(1, DV_D)
        gn_r = gn_ret[layer].astype(F32).reshape(1, DV_R)

        k_p, v_p, kb_p, vb_p, q_p, gd_p, qr_p, kr_p, vr_p, gr_p = _in_projection(
            yp, g_pre, w_in_b, cos_p, sin_p, tm_p)
        od_p = _prompt_attention(lam, q_p, kb_p, vb_p, tabs_p, batch, seq, t_attn)
        or_p, s_p = _prompt_retention(qr_p, kr_p, vr_p, batch, seq)
        yp = _merge(yp, od_p, or_p, gd_p, gr_p, gn_d, gn_r, w_out_b, g_post, tm_p)
        kp_l.append(k_p.reshape(batch, seq, H_D, 2 * D_QK))
        vp_l.append(v_p.reshape(batch, seq, H_D, DV_D))
        sp_l.append(s_p)

        k_s, v_s, _, _, q_s, gd_s, qr_s, kr_s, vr_s, gr_s = _in_projection(
            ys, g_pre, w_in_b, cos_s, sin_s, tm_s)
        q4 = q_s.astype(F32).reshape(dec_batch, t_new, H_D, 2 * D_QK).transpose(0, 2, 1, 3)
        q_rows = jnp.concatenate([q4, q4], axis=2).reshape(dec_batch, H_D * 2 * t_new, 2 * D_QK)
        od_s = _decode_attention(page_table + layer * n_pool, lam, q_rows,
                                 k_s.reshape(dec_batch, t_new * H_D, 2 * D_QK),
                                 v_s.reshape(dec_batch, t_new * H_D, DV_D),
                                 tabs_s.reshape(3, H_D * 2 * t_new, page * H_D),
                                 cache_k.reshape(depth * n_pool, page * H_D, 2 * D_QK),
                                 cache_v.reshape(depth * n_pool, page * H_D, DV_D))
        od_s = od_s.reshape(dec_batch, H_D, 2, t_new, DV_D)[:, :, 0].transpose(0, 2, 1, 3)
        od_s = od_s.reshape(dec_batch * t_new, W_D)
        to_cols = lambda a: a.astype(F32).reshape(dec_batch, t_new, H_R, DK_R).transpose(0, 2, 3, 1)
        or_s, s_s = _sample_retention(to_cols(qr_s), to_cols(kr_s),
                                      vr_s.astype(F32).reshape(dec_batch, t_new, W_R),
                                      state_ret[layer].astype(F32))
        ys = _merge(ys, od_s, or_s.reshape(dec_batch * t_new, W_R), gd_s, gr_s, gn_d, gn_r, w_out_b, g_post, tm_s)
        ks_l.append(k_s.reshape(dec_batch, t_new, H_D, 2 * D_QK))
        vs_l.append(v_s.reshape(dec_batch, t_new, H_D, DV_D))
        ss_l.append(s_s)

    stack = lambda parts: parts[0][None] if len(parts) == 1 else jnp.stack(parts)
    return (yp.reshape(batch, seq, d_model), ys.reshape(dec_batch, t_new, d_model),
            stack(kp_l), stack(vp_l), stack(sp_l), stack(ks_l), stack(vs_l), stack(ss_l))
```

```python
import functools
import math

import jax
import jax.numpy as jnp
import numpy as np
from jax import lax
from jax.experimental import pallas as pl
from jax.experimental.pallas import tpu as pltpu

F32 = jnp.float32
BF16 = jnp.bfloat16

H_D = 4
D_QK = 64
DV_D = 128
W_D = H_D * DV_D
H_R = 4
DK_R = 64
DV_R = 128
W_R = H_R * DV_R
NUM_BUCKETS = 32
MAX_DISTANCE = 128
RET_CHUNK = 128
EPS = 1e-6
NEG = -1e30
LOG2E = 1.4426950408889634
Q_PRESCALE = D_QK ** -0.5 * LOG2E
MASK_CODE = NUM_BUCKETS
LANES = 128
VMEM_LIMIT = 56 * 1024 * 1024

_SIZES = [H_D * 2 * D_QK, H_D * 2 * D_QK, W_D, W_D, H_R * DK_R, H_R * DK_R, W_R, W_R]
_OFF = [0] + [int(v) for v in np.cumsum(_SIZES)]


def _cparams(n_axes):
    return pltpu.CompilerParams(dimension_semantics=("arbitrary",) * n_axes,
                                vmem_limit_bytes=VMEM_LIMIT)


def _bucket_np(rel):
    n = np.maximum(rel, 0)
    max_exact = NUM_BUCKETS // 2
    nf = np.maximum(n, max_exact).astype(np.float32)
    large = max_exact + (np.log(nf / np.float32(max_exact)) / np.float32(math.log(MAX_DISTANCE / max_exact))
                         * np.float32(NUM_BUCKETS - max_exact)).astype(np.int32)
    large = np.minimum(large, NUM_BUCKETS - 1)
    return np.where(n < max_exact, n, large).astype(np.int32)


def _far_distance():
    b = _bucket_np(np.arange(4 * MAX_DISTANCE))
    return int(np.max(np.nonzero(b != NUM_BUCKETS - 1)[0])) + 1


def _prompt_bucket_codes(t):
    r = np.arange(t)[:, None]
    c = np.arange(t)[None, :]
    far = np.full((t, t), NUM_BUCKETS - 1)
    left = _bucket_np(r - c + t)
    diag = np.where(r >= c, _bucket_np(r - c), MASK_CODE)
    return np.stack([far, left, diag]).astype(np.int32)[:, None]


def _decode_bucket_codes(page, t_new):
    h = np.arange(H_D)[:, None, None]
    t = np.tile(np.arange(t_new), 2)[None, :, None]
    col = np.arange(page * H_D)[None, None, :]
    j, hk = col // H_D, col % H_D
    own = hk == h
    far = np.where(own, NUM_BUCKETS - 1, MASK_CODE) + 0 * t
    last = np.where(own, _bucket_np(page + t - j), MASK_CODE)
    new = np.where(own & (j <= t) & (j < t_new), _bucket_np(t - j), MASK_CODE)
    return np.stack([far, last, new]).astype(np.int32)


def _rotation_tables(pos):
    half = DK_R // 2
    inv = 1.0 / (10000.0 ** jnp.linspace(0.0, 1.0, half, dtype=F32))
    ang = pos.astype(F32)[:, None] * inv[None, :]
    cos = jnp.cos(ang)
    sin = jnp.sin(ang)
    cos_t = jnp.tile(cos, (1, LANES // half))
    sin_t = jnp.tile(jnp.concatenate([-sin, sin], axis=1), (1, LANES // DK_R))
    return cos_t, sin_t


def _log_gamma_np():
    return np.log(1.0 - 2.0 ** (-5.0 - np.arange(H_R, dtype=np.float64)))


def _table_kernel(vals_ref, code_ref, out_ref):
    h = pl.program_id(1)
    code = code_ref[0, 0]
    acc = jnp.full(code.shape, vals_ref[MASK_CODE, h], F32)
    for b in range(NUM_BUCKETS):
        acc = jnp.where(code == b, vals_ref[b, h], acc)
    out_ref[0, 0] = acc


def _bias_tables(vals, codes):
    n, hc, r, c = codes.shape
    code_map = (lambda t, h: (t, h, 0, 0)) if hc == H_D else (lambda t, h: (t, 0, 0, 0))
    return pl.pallas_call(
        _table_kernel,
        grid=(n, H_D),
        in_specs=[pl.BlockSpec(memory_space=pltpu.SMEM),
                  pl.BlockSpec((1, 1, r, c), code_map)],
        out_specs=pl.BlockSpec((1, 1, r, c), lambda t, h: (t, h, 0, 0)),
        out_shape=jax.ShapeDtypeStruct((n, H_D, r, c), F32),
        compiler_params=_cparams(2),
        name="bias_tables",
    )(vals, codes)


def _inproj_kernel(x_ref, g_ref, w_ref, cos_ref, sin_ref,
                   k_ref, v_ref, kb_ref, vb_ref, q_ref, gd_ref, qr_ref, kr_ref, vr_ref, gr_ref):
    tm = x_ref.shape[0]
    x = x_ref[...]
    inv = lax.rsqrt(jnp.mean(x * x, axis=-1, keepdims=True) + EPS)
    h = (x * inv * g_ref[...]).astype(BF16)

    def proj(i):
        return jnp.dot(h, w_ref[:, _OFF[i]:_OFF[i + 1]], preferred_element_type=F32)

    def store_heads_as_rows(ref, z):
        for hd in range(H_D):
            ref[pl.ds(hd, tm, stride=H_D), :] = z[:, hd * LANES:(hd + 1) * LANES]

    q_ref[...] = (proj(0) * Q_PRESCALE).astype(BF16)
    zk = proj(1)
    store_heads_as_rows(k_ref, zk)
    kb_ref[...] = zk.astype(BF16)
    zv = proj(2)
    store_heads_as_rows(v_ref, zv)
    vb_ref[...] = zv.astype(BF16)
    gd_ref[...] = proj(3).astype(BF16)

    cos = cos_ref[...]
    sin = sin_ref[...]
    lane = lax.broadcasted_iota(jnp.int32, cos.shape, 1)
    low_half = (lane % DK_R) < (DK_R // 2)

    def rotate(z):
        outs = []
        for c in range(z.shape[1] // LANES):
            zc = z[:, c * LANES:(c + 1) * LANES]
            partner = jnp.where(low_half,
                                pltpu.roll(zc, LANES - DK_R // 2, 1),
                                pltpu.roll(zc, DK_R // 2, 1))
            outs.append(zc * cos + partner * sin)
        return jnp.concatenate(outs, axis=1)

    qr_ref[...] = rotate(proj(4)).astype(BF16)
    kr_ref[...] = (rotate(proj(5)) * (DK_R ** -0.5)).astype(BF16)
    vr_ref[...] = proj(6).astype(BF16)
    gr_ref[...] = proj(7).astype(BF16)


def _in_projection(x2d, g_pre, w_bf16, cos_t, sin_t, tm):
    n, d = x2d.shape
    n_tab = cos_t.shape[0] // tm
    row = lambda i: (i, 0)
    tab = lambda i: (i % n_tab, 0)
    outs = [(H_D, 2 * D_QK, F32), (H_D, DV_D, F32), (1, W_D, BF16), (1, W_D, BF16),
            (1, 2 * H_D * D_QK, BF16), (1, W_D, BF16),
            (1, H_R * DK_R, BF16), (1, H_R * DK_R, BF16), (1, W_R, BF16), (1, W_R, BF16)]
    return pl.pallas_call(
        _inproj_kernel,
        grid=(n // tm,),
        in_specs=[pl.BlockSpec((tm, d), row),
                  pl.BlockSpec((1, d), lambda i: (0, 0)),
                  pl.BlockSpec(w_bf16.shape, lambda i: (0, 0)),
                  pl.BlockSpec((tm, LANES), tab),
                  pl.BlockSpec((tm, LANES), tab)],
        out_specs=[pl.BlockSpec((tm * r, w), row) for r, w, _ in outs],
        out_shape=[jax.ShapeDtypeStruct((n * r, w), dt) for r, w, dt in outs],
        compiler_params=_cparams(1),
        name="in_projection",
    )(x2d, g_pre, w_bf16, cos_t, sin_t)


def _attn_kernel(lam_ref, qi_ref, kj_ref, q_ref, kb, v_ref, tab_ref, o_ref, vb, s_a, s_b, m_sc, acc_sc,
                 *, t, n_pairs):
    s_len = kb.shape[0]
    vb[:, :DV_D] = v_ref[...]
    vb[:, DV_D:] = jnp.ones((s_len, DV_D), BF16)
    m_sc[...] = jnp.full(m_sc.shape, NEG, F32)
    acc_sc[...] = jnp.zeros(acc_sc.shape, F32)
    lane = lax.broadcasted_iota(jnp.int32, (t, 2 * D_QK), 1)

    def scores_into(s_ref, n):
        q = q_ref[pl.ds(pl.multiple_of(qi_ref[n] * t, t), t), :]
        zero = jnp.zeros_like(q)
        qs = jnp.concatenate([jnp.where(lane < D_QK, q, zero), jnp.where(lane >= D_QK, q, zero)], axis=0)
        keys = kb[pl.ds(pl.multiple_of(kj_ref[n] * t, t), t), :]
        tab = tab_ref[jnp.clip(kj_ref[n] - qi_ref[n] + 2, 0, 2), 0]
        s_ref[...] = (lax.dot_general(qs, keys, (((1,), (1,)), ((), ())), preferred_element_type=F32)
                      + jnp.concatenate([tab, tab], axis=0))

    def consume(s_ref, n):
        i = qi_ref[n]
        j = kj_ref[n]
        s = s_ref[...]
        m_prev = m_sc[...]
        m_new = jnp.maximum(m_prev, jnp.max(s, axis=1, keepdims=True))
        corr = jnp.exp2(m_prev - m_new)
        p = jnp.exp2(s - jnp.concatenate([m_new] * (t // LANES), axis=1))
        pv = jnp.dot(p.astype(BF16), vb[pl.ds(pl.multiple_of(j * t, t), t), :], preferred_element_type=F32)
        acc_sc[...] = acc_sc[...] * jnp.concatenate([corr, corr], axis=1) + pv
        m_sc[...] = m_new

        @pl.when(j == i)
        def _():
            acc = acc_sc[...]
            nrm = acc[:, :DV_D] / acc[:, DV_D:]
            o_ref[pl.ds(pl.multiple_of(i * t, t), t), :] = (nrm[:t] - lam_ref[0] * nrm[t:]).astype(o_ref.dtype)
            m_sc[...] = jnp.full(m_sc.shape, NEG, F32)
            acc_sc[...] = jnp.zeros(acc_sc.shape, F32)

    scores_into(s_a, 0)

    def two_pairs(h, carry):
        n = 2 * h
        scores_into(s_b, n + 1)
        consume(s_a, n)
        scores_into(s_a, n + 2)
        consume(s_b, n + 1)
        return carry

    lax.fori_loop(0, n_pairs // 2, two_pairs, 0)
    if n_pairs % 2:
        consume(s_a, n_pairs - 1)


def _prompt_attention(lam, q2d, k2d, v2d, tabs, batch, seq, t):
    nq = seq // t
    pairs = [(i, j) for i in range(nq) for j in range(i + 1)]
    pairs.append(pairs[-1])
    qi = jnp.asarray([p[0] for p in pairs], jnp.int32)
    kj = jnp.asarray([p[1] for p in pairs], jnp.int32)
    bh = lambda b, h: (b, h)
    smem = pl.BlockSpec(memory_space=pltpu.SMEM)
    return pl.pallas_call(
        functools.partial(_attn_kernel, t=t, n_pairs=len(pairs) - 1),
        grid=(batch, H_D),
        in_specs=[smem, smem, smem,
                  pl.BlockSpec((seq, 2 * D_QK), bh),
                  pl.BlockSpec((seq, 2 * D_QK), bh),
                  pl.BlockSpec((seq, DV_D), bh),
                  pl.BlockSpec((3, 1, t, t), lambda b, h: (0, h, 0, 0))],
        out_specs=pl.BlockSpec((seq, DV_D), bh),
        out_shape=jax.ShapeDtypeStruct((batch * seq, W_D), BF16),
        scratch_shapes=[pltpu.VMEM((seq, 2 * DV_D), BF16),
                        pltpu.VMEM((2 * t, t), F32),
                        pltpu.VMEM((2 * t, t), F32),
                        pltpu.VMEM((2 * t, LANES), F32),
                        pltpu.VMEM((2 * t, 2 * DV_D), F32)],
        compiler_params=_cparams(2),
        name="prompt_attention",
    )(lam, qi, kj, q2d, k2d, v2d, tabs)


def _ret_kernel(qr_ref, kr_ref, vr_ref, dec_ref, qdec_ref, kdec_ref, o_ref, s_out_ref, st_sc, *, g_chunk, bb):
    c = pl.program_id(1)
    chunk = qr_ref.shape[1]

    @pl.when(c == 0)
    def _():
        st_sc[...] = jnp.zeros(st_sc.shape, F32)

    head_of_lane = lax.broadcasted_iota(jnp.int32, (chunk, H_R * DK_R), 1) // DK_R
    for i in range(bb):
        q = qr_ref[i]
        k = kr_ref[i]
        v = vr_ref[i]
        st = st_sc[i]
        q_all = jnp.concatenate([jnp.where(head_of_lane == h, q, jnp.zeros_like(q)) for h in range(H_R)], axis=0)
        sc_all = lax.dot_general(q_all, k, (((1,), (1,)), ((), ())), preferred_element_type=F32) * dec_ref[...]
        cross_all = jnp.dot(q_all, st.astype(BF16), preferred_element_type=F32) * qdec_ref[...]
        kd_t = (k.astype(F32) * kdec_ref[...]).T
        for h in range(H_R):
            tok = slice(h * chunk, (h + 1) * chunk)
            rows = slice(h * DK_R, (h + 1) * DK_R)
            vh = v[:, h * DV_R:(h + 1) * DV_R]
            inner = jnp.dot(sc_all[tok].astype(BF16), vh, preferred_element_type=F32)
            o_ref[i, :, h * DV_R:(h + 1) * DV_R] = (inner + cross_all[tok]).astype(o_ref.dtype)
            upd = jnp.dot(kd_t[rows, :].astype(BF16), vh, preferred_element_type=F32)
            st_new = g_chunk[h] * st[rows, :] + upd
            st_sc[i, rows, :] = st_new
            s_out_ref[i, h] = st_new


def _prompt_retention(qr, kr, vr, batch, seq, bb=4):
    c = min(RET_CHUNK, seq)
    nc = seq // c
    bb = math.gcd(bb, batch)
    lg = _log_gamma_np()
    idx = np.arange(c, dtype=np.float64)
    d = idx[:, None] - idx[None, :]
    decay = np.where(d[None] >= 0, np.exp(lg[:, None, None] * np.maximum(d, 0.0)[None]), 0.0)
    qdec = np.broadcast_to(np.exp(lg[:, None] * (idx + 1.0)[None, :])[:, :, None], (H_R, c, DV_R))
    kdec = np.repeat(np.exp(lg[:, None] * (c - 1.0 - idx)[None, :]).T, DK_R, axis=1)
    g_chunk = tuple(float(v) for v in np.exp(lg * c))
    blk = lambda w: pl.BlockSpec((bb, c, w), lambda b, j: (b, j, 0))
    fix = lambda b, j: (0, 0)
    as3 = lambda a: a.reshape(batch, seq, a.shape[-1])
    o, s = pl.pallas_call(
        functools.partial(_ret_kernel, g_chunk=g_chunk, bb=bb),
        grid=(batch // bb, nc),
        in_specs=[blk(H_R * DK_R), blk(H_R * DK_R), blk(W_R),
                  pl.BlockSpec((H_R * c, c), fix),
                  pl.BlockSpec((H_R * c, DV_R), fix),
                  pl.BlockSpec((c, H_R * DK_R), fix)],
        out_specs=[blk(W_R),
                   pl.BlockSpec((bb, H_R, DK_R, DV_R), lambda b, j: (b, 0, 0, 0))],
        out_shape=[jax.ShapeDtypeStruct((batch, seq, W_R), BF16),
                   jax.ShapeDtypeStruct((batch, H_R, DK_R, DV_R), F32)],
        scratch_shapes=[pltpu.VMEM((bb, H_R * DK_R, DV_R), F32)],
        compiler_params=_cparams(2),
        name="prompt_retention",
    )(as3(qr), as3(kr), as3(vr), jnp.asarray(decay.reshape(H_R * c, c), F32),
      jnp.asarray(qdec.reshape(H_R * c, DV_R), F32), jnp.asarray(kdec, F32))
    return o.reshape(batch * seq, W_R), s


def _decode_kernel(pt_ref, lam_ref, q_ref, kn_ref, vn_ref, tab_ref, ck_hbm, cv_hbm, o_ref,
                   kbuf, vbuf, knew, vnew, m_sc, l_sc, acc_sc, sem, *, n_batch, n_pages, group, n_slot, t_new):
    n_group = n_pages // group
    rows = H_D * 2 * t_new
    width = kbuf.shape[2]

    def page_copies(b, g, slot, real):
        out = []
        for p in range(group):
            phys = pt_ref[b * n_pages + g * group + p] if real else 0
            out.append(pltpu.make_async_copy(ck_hbm.at[phys], kbuf.at[slot, p], sem.at[0, slot]))
            out.append(pltpu.make_async_copy(cv_hbm.at[phys], vbuf.at[slot, p], sem.at[1, slot]))
        return out

    def start_group(b, g, slot):
        for cp in page_copies(b, g, slot, True):
            cp.start()

    b = pl.program_id(0)
    g = pl.program_id(1)

    @pl.when((b == 0) & (g == 0))
    def _():
        for g0 in range(min(n_slot, n_group)):
            start_group(0, g0, g0 % n_slot)
        knew[...] = jnp.zeros(knew.shape, F32)
        vnew[...] = jnp.zeros(vnew.shape, F32)

    row = lax.broadcasted_iota(jnp.int32, (rows, 2 * D_QK), 0)
    lane = lax.broadcasted_iota(jnp.int32, (rows, 2 * D_QK), 1)
    own_half = ((row // t_new) % 2 == 0) == (lane < D_QK)
    q = jnp.where(own_half, q_ref[0], 0.0).astype(BF16)

    def softmax_update(state, scores, values):
        m, l, acc = state
        mx = scores[0]
        for s in scores[1:]:
            mx = jnp.maximum(mx, s)
        m_new = jnp.maximum(m, jnp.broadcast_to(jnp.max(mx, axis=1, keepdims=True), m.shape))
        corr = jnp.exp2(m - m_new)
        m_wide = jnp.concatenate([m_new] * (width // LANES), axis=1)
        ps = [jnp.exp2(s - m_wide) for s in scores]
        psum = ps[0]
        for p in ps[1:]:
            psum = psum + p
        l_new = l * corr + jnp.broadcast_to(jnp.sum(psum, axis=1, keepdims=True), l.shape)
        pv = jnp.dot(ps[0].astype(BF16), values[0], preferred_element_type=F32)
        for p, vv in zip(ps[1:], values[1:]):
            pv = pv + jnp.dot(p.astype(BF16), vv, preferred_element_type=F32)
        return m_new, l_new, acc * corr + pv

    def qk(keys):
        return lax.dot_general(q, keys, (((1,), (1,)), ((), ())), preferred_element_type=F32)

    @pl.when(g == 0)
    def _():
        m_sc[...] = jnp.full(m_sc.shape, NEG, F32)
        l_sc[...] = jnp.zeros(l_sc.shape, F32)
        acc_sc[...] = jnp.zeros(acc_sc.shape, F32)

    slot = lax.rem(g, n_slot)
    for cp in page_copies(b, g, slot, False):
        cp.wait()
    scores = [qk(kbuf[slot, p].astype(BF16)) + tab_ref[0] for p in range(group - 1)]
    scores.append(qk(kbuf[slot, group - 1].astype(BF16)) + tab_ref[jnp.where(g == n_group - 1, 1, 0)])
    values = [vbuf[slot, p].astype(BF16) for p in range(group)]
    m_sc[...], l_sc[...], acc_sc[...] = softmax_update((m_sc[...], l_sc[...], acc_sc[...]), scores, values)

    nxt = b * n_group + g + n_slot

    @pl.when(nxt < n_batch * n_group)
    def _():
        start_group(nxt // n_group, lax.rem(nxt, n_group), slot)

    @pl.when(g == n_group - 1)
    def _():
        n_new = kn_ref.shape[1]
        knew[0:n_new, :] = kn_ref[0]
        vnew[0:n_new, :] = vn_ref[0]
        s_new = qk(knew[...].astype(BF16)) + tab_ref[2]
        _, l, acc = softmax_update((m_sc[...], l_sc[...], acc_sc[...]), [s_new], [vnew[...].astype(BF16)])
        n = acc / l
        o_ref[0] = n - lam_ref[0] * pltpu.roll(n, rows - t_new, 0)


def _decode_attention(page_table, lam, q_rows, k_new, v_new, tabs, cache_k, cache_v, group=8, n_slot=4):
    n_batch, rows, _ = q_rows.shape
    n_pages = page_table.shape[1]
    width = cache_k.shape[1]
    group = math.gcd(group, n_pages)
    n_slot = max(1, math.gcd(n_slot, n_pages // group))
    per_b = lambda a: pl.BlockSpec((1,) + a.shape[1:], lambda b, g: (b, 0, 0))
    return pl.pallas_call(
        functools.partial(_decode_kernel, n_batch=n_batch, n_pages=n_pages, group=group,
                          n_slot=n_slot, t_new=rows // (2 * H_D)),
        grid=(n_batch, n_pages // group),
        in_specs=[pl.BlockSpec(memory_space=pltpu.SMEM),
                  pl.BlockSpec(memory_space=pltpu.SMEM),
                  per_b(q_rows), per_b(k_new), per_b(v_new),
                  pl.BlockSpec(tabs.shape, lambda b, g: (0, 0, 0)),
                  pl.BlockSpec(memory_space=pl.ANY),
                  pl.BlockSpec(memory_space=pl.ANY)],
        out_specs=per_b(q_rows),
        out_shape=jax.ShapeDtypeStruct(q_rows.shape, F32),
        scratch_shapes=[pltpu.VMEM((n_slot, group, width, LANES), F32),
                        pltpu.VMEM((n_slot, group, width, LANES), F32),
                        pltpu.VMEM((width, LANES), F32),
                        pltpu.VMEM((width, LANES), F32),
                        pltpu.VMEM((rows, LANES), F32),
                        pltpu.VMEM((rows, LANES), F32),
                        pltpu.VMEM((rows, DV_D), F32),
                        pltpu.SemaphoreType.DMA((2, n_slot))],
        compiler_params=_cparams(2),
        name="decode_attention",
    )(page_table.reshape(-1), lam, q_rows, k_new, v_new, tabs, cache_k, cache_v)


def _sample_ret_kernel(qt_ref, kt_ref, v_ref, s_ref, o_ref, s_out_ref, *, gamma, bb, t_new):
    for i in range(bb):
        v = v_ref[i]
        for h in range(H_R):
            st = s_ref[i, h]
            qt = qt_ref[i, h]
            kt = kt_ref[i, h]
            outs = []
            for t in range(t_new):
                vrow = v[t:t + 1, h * DV_R:(h + 1) * DV_R]
                st = gamma[h] * st + kt[:, t:t + 1] * vrow
                outs.append(jnp.sum(qt[:, t:t + 1] * st, axis=0, keepdims=True))
            o_ref[i, :, h * DV_R:(h + 1) * DV_R] = jnp.concatenate(outs, axis=0)
            s_out_ref[i, h] = st


def _sample_retention(qr_t, kr_t, vr, state, bb=8):
    n_batch, _, _, t_new = qr_t.shape
    bb = math.gcd(bb, n_batch)
    gamma = tuple(float(v) for v in np.exp(_log_gamma_np()))
    m4 = lambda i: (i, 0, 0, 0)
    m3 = lambda i: (i, 0, 0)
    return pl.pallas_call(
        functools.partial(_sample_ret_kernel, gamma=gamma, bb=bb, t_new=t_new),
        grid=(n_batch // bb,),
        in_specs=[pl.BlockSpec((bb, H_R, DK_R, t_new), m4),
                  pl.BlockSpec((bb, H_R, DK_R, t_new), m4),
                  pl.BlockSpec((bb, t_new, W_R), m3),
                  pl.BlockSpec((bb, H_R, DK_R, DV_R), m4)],
        out_specs=[pl.BlockSpec((bb, t_new, W_R), m3),
                   pl.BlockSpec((bb, H_R, DK_R, DV_R), m4)],
        out_shape=[jax.ShapeDtypeStruct((n_batch, t_new, W_R), F32),
                   jax.ShapeDtypeStruct(state.shape, F32)],
        compiler_params=_cparams(1),
        name="sample_retention",
    )(qr_t, kr_t, vr, state)


def _merge_kernel(x_ref, od_ref, or_ref, gd_ref, gr_ref, gnd_ref, gnr_ref, w_ref, gpost_ref, y_ref):
    def gated(o_ref_, g_ref_, gn):
        parts = []
        for h in range(o_ref_.shape[1] // LANES):
            cols = slice(h * LANES, (h + 1) * LANES)
            o = o_ref_[:, cols].astype(F32)
            g = g_ref_[:, cols].astype(F32)
            y = o * lax.rsqrt(jnp.mean(o * o, axis=-1, keepdims=True) + EPS) * gn
            parts.append((y * (g * jax.nn.sigmoid(g))).astype(BF16))
        return parts

    o = jnp.concatenate(gated(od_ref, gd_ref, gnd_ref[...]) + gated(or_ref, gr_ref, gnr_ref[...]), axis=1)
    mix = jnp.dot(o, w_ref[...], preferred_element_type=F32)
    y = mix * lax.rsqrt(jnp.mean(mix * mix, axis=-1, keepdims=True) + EPS) * gpost_ref[...]
    y_ref[...] = x_ref[...] + y


def _merge(x2d, od, orr, gd, gr, gn_d, gn_r, w_bf16, g_post, tm):
    n, d = x2d.shape
    row = lambda i: (i, 0)
    fix = lambda i: (0, 0)
    return pl.pallas_call(
        _merge_kernel,
        grid=(n // tm,),
        in_specs=[pl.BlockSpec((tm, d), row),
                  pl.BlockSpec((tm, W_D), row),
                  pl.BlockSpec((tm, W_R), row),
                  pl.BlockSpec((tm, W_D), row),
                  pl.BlockSpec((tm, W_R), row),
                  pl.BlockSpec((1, DV_D), fix),
                  pl.BlockSpec((1, DV_R), fix),
                  pl.BlockSpec(w_bf16.shape, fix),
                  pl.BlockSpec((1, d), fix)],
        out_specs=pl.BlockSpec((tm, d), row),
        out_shape=jax.ShapeDtypeStruct((n, d), F32),
        compiler_params=_cparams(1),
        name="merge_out_projection",
    )(x2d, od, orr, gd, gr, gn_d, gn_r, w_bf16, g_post)


def _pick_tile(n, target):
    t = math.gcd(n, target)
    assert t % 8 == 0, (n, target)
    return t


def kernel(x_prompt, x_sample, cache_k, cache_v, state_ret, page_table, rel_bias, norm_pre, norm_post,
           w_in, w_out, lambda_q1, lambda_k1, lambda_q2, lambda_k2, gn_diff, gn_ret):
    depth = w_in.shape[0]
    batch, seq, d_model = x_prompt.shape
    dec_batch, t_new, _ = x_sample.shape
    n_pool, page = cache_k.shape[1], cache_k.shape[2]
    n_pages = page_table.shape[1]
    past = n_pages * page
    far = _far_distance()

    t_attn = _pick_tile(seq, 512)
    assert t_attn > far and page > far, "bias must be constant beyond the neighbouring tile / page"
    tm_p = _pick_tile(batch * seq, 512)
    assert seq % tm_p == 0
    tm_s = _pick_tile(dec_batch * t_new, 512)

    vals = jnp.concatenate([(rel_bias - rel_bias[NUM_BUCKETS - 1:]) * LOG2E,
                            jnp.full((1, H_D), NEG, F32)], axis=0).astype(F32)
    tabs_p = _bias_tables(vals, jnp.asarray(_prompt_bucket_codes(t_attn)))
    tabs_s = _bias_tables(vals, jnp.asarray(_decode_bucket_codes(page, t_new)))

    cos_p, sin_p = _rotation_tables(jnp.arange(seq, dtype=jnp.int32))
    pos_s = past + (jnp.arange(tm_s, dtype=jnp.int32) % t_new)
    cos_s, sin_s = _rotation_tables(pos_s)

    yp = x_prompt.reshape(batch * seq, d_model)
    ys = x_sample.reshape(dec_batch * t_new, d_model)
    kp_l, vp_l, sp_l, ks_l, vs_l, ss_l = [], [], [], [], [], []
    for layer in range(depth):
        lambda_init = 0.8 - 0.6 * math.exp(-0.3 * layer)
        lam = (jnp.exp(jnp.sum(lambda_q1[layer].astype(F32) * lambda_k1[layer].astype(F32)))
               - jnp.exp(jnp.sum(lambda_q2[layer].astype(F32) * lambda_k2[layer].astype(F32)))
               + lambda_init).reshape(1).astype(F32)
        w_in_b = w_in[layer].astype(BF16)
        w_out_b = w_out[layer].astype(BF16)
        g_pre = norm_pre[layer].reshape(1, d_model)
        g_post = norm_post[layer].reshape(1, d_model)
        gn_d = (gn_diff[layer].astype(F32) * (1.0 - lambda_init)).reshape(1, DV_D)
        gn_r = gn_ret[layer].astype(F32).reshape(1, DV_R)

        k_p, v_p, kb_p, vb_p, q_p, gd_p, qr_p, kr_p, vr_p, gr_p = _in_projection(
            yp, g_pre, w_in_b, cos_p, sin_p, tm_p)
        od_p = _prompt_attention(lam, q_p, kb_p, vb_p, tabs_p, batch, seq, t_attn)
        or_p, s_p = _prompt_retention(qr_p, kr_p, vr_p, batch, seq)
        yp = _merge(yp, od_p, or_p, gd_p, gr_p, gn_d, gn_r, w_out_b, g_post, tm_p)
        kp_l.append(k_p.reshape(batch, seq, H_D, 2 * D_QK))
        vp_l.append(v_p.reshape(batch, seq, H_D, DV_D))
        sp_l.append(s_p)

        k_s, v_s, _, _, q_s, gd_s, qr_s, kr_s, vr_s, gr_s = _in_projection(
            ys, g_pre, w_in_b, cos_s, sin_s, tm_s)
        q4 = q_s.astype(F32).reshape(dec_batch, t_new, H_D, 2 * D_QK).transpose(0, 2, 1, 3)
        q_rows = jnp.concatenate([q4, q4], axis=2).reshape(dec_batch, H_D * 2 * t_new, 2 * D_QK)
        od_s = _decode_attention(page_table + layer * n_pool, lam, q_rows,
                                 k_s.reshape(dec_batch, t_new * H_D, 2 * D_QK),
                                 v_s.reshape(dec_batch, t_new * H_D, DV_D),
                                 tabs_s.reshape(3, H_D * 2 * t_new, page * H_D),
                                 cache_k.reshape(depth * n_pool, page * H_D, 2 * D_QK),
                                 cache_v.reshape(depth * n_pool, page * H_D, DV_D))
        od_s = od_s.reshape(dec_batch, H_D, 2, t_new, DV_D)[:, :, 0].transpose(0, 2, 1, 3)
        od_s = od_s.reshape(dec_batch * t_new, W_D)
        to_cols = lambda a: a.astype(F32).reshape(dec_batch, t_new, H_R, DK_R).transpose(0, 2, 3, 1)
        or_s, s_s = _sample_retention(to_cols(qr_s), to_cols(kr_s),
                                      vr_s.astype(F32).reshape(dec_batch, t_new, W_R),
                                      state_ret[layer].astype(F32))
        ys = _merge(ys, od_s, or_s.reshape(dec_batch * t_new, W_R), gd_s, gr_s, gn_d, gn_r, w_out_b, g_post, tm_s)
        ks_l.append(k_s.reshape(dec_batch, t_new, H_D, 2 * D_QK))
        vs_l.append(v_s.reshape(dec_batch, t_new, H_D, DV_D))
        ss_l.append(s_s)

    stack = lambda parts: parts[0][None] if len(parts) == 1 else jnp.stack(parts)
    return (yp.reshape(batch, seq, d_model), ys.reshape(dec_batch, t_new, d_model),
            stack(kp_l), stack(vp_l), stack(sp_l), stack(ks_l), stack(vs_l), stack(ss_l))
```

```python
import functools
import math

import jax
import jax.numpy as jnp
import numpy as np
from jax import lax
from jax.experimental import pallas as pl
from jax.experimental.pallas import tpu as pltpu

F32 = jnp.float32
BF16 = jnp.bfloat16

H_D = 4
D_QK = 64
DV_D = 128
W_D = H_D * DV_D
H_R = 4
DK_R = 64
DV_R = 128
W_R = H_R * DV_R
NUM_BUCKETS = 32
MAX_DISTANCE = 128
RET_CHUNK = 128
EPS = 1e-6
NEG = -1e30
LOG2E = 1.4426950408889634
Q_PRESCALE = D_QK ** -0.5 * LOG2E
MASK_CODE = NUM_BUCKETS
LANES = 128
VMEM_LIMIT = 56 * 1024 * 1024

_SIZES = [H_D * 2 * D_QK, H_D * 2 * D_QK, W_D, W_D, H_R * DK_R, H_R * DK_R, W_R, W_R]
_OFF = [0] + [int(v) for v in np.cumsum(_SIZES)]


def _cparams(n_axes):
    return pltpu.CompilerParams(dimension_semantics=("arbitrary",) * n_axes,
                                vmem_limit_bytes=VMEM_LIMIT)


def _bucket_np(rel):
    n = np.maximum(rel, 0)
    max_exact = NUM_BUCKETS // 2
    nf = np.maximum(n, max_exact).astype(np.float32)
    large = max_exact + (np.log(nf / np.float32(max_exact)) / np.float32(math.log(MAX_DISTANCE / max_exact))
                         * np.float32(NUM_BUCKETS - max_exact)).astype(np.int32)
    large = np.minimum(large, NUM_BUCKETS - 1)
    return np.where(n < max_exact, n, large).astype(np.int32)


def _far_distance():
    b = _bucket_np(np.arange(4 * MAX_DISTANCE))
    return int(np.max(np.nonzero(b != NUM_BUCKETS - 1)[0])) + 1


def _prompt_bucket_codes(t):
    r = np.arange(t)[:, None]
    c = np.arange(t)[None, :]
    far = np.full((t, t), NUM_BUCKETS - 1)
    left = _bucket_np(r - c + t)
    diag = np.where(r >= c, _bucket_np(r - c), MASK_CODE)
    return np.stack([far, left, diag]).astype(np.int32)[:, None]


def _decode_bucket_codes(page, t_new):
    h = np.arange(H_D)[:, None, None]
    t = np.tile(np.arange(t_new), 2)[None, :, None]
    col = np.arange(page * H_D)[None, None, :]
    j, hk = col // H_D, col % H_D
    own = hk == h
    far = np.where(own, NUM_BUCKETS - 1, MASK_CODE) + 0 * t
    last = np.where(own, _bucket_np(page + t - j), MASK_CODE)
    new = np.where(own & (j <= t) & (j < t_new), _bucket_np(t - j), MASK_CODE)
    return np.stack([far, last, new]).astype(np.int32)


def _rotation_tables(pos):
    half = DK_R // 2
    inv = 1.0 / (10000.0 ** jnp.linspace(0.0, 1.0, half, dtype=F32))
    ang = pos.astype(F32)[:, None] * inv[None, :]
    cos = jnp.cos(ang)
    sin = jnp.sin(ang)
    cos_t = jnp.tile(cos, (1, LANES // half))
    sin_t = jnp.tile(jnp.concatenate([-sin, sin], axis=1), (1, LANES // DK_R))
    return cos_t, sin_t


def _log_gamma_np():
    return np.log(1.0 - 2.0 ** (-5.0 - np.arange(H_R, dtype=np.float64)))


def _table_kernel(vals_ref, code_ref, out_ref):
    h = pl.program_id(1)
    code = code_ref[0, 0]
    acc = jnp.full(code.shape, vals_ref[MASK_CODE, h], F32)
    for b in range(NUM_BUCKETS):
        acc = jnp.where(code == b, vals_ref[b, h], acc)
    out_ref[0, 0] = acc


def _bias_tables(vals, codes):
    n, hc, r, c = codes.shape
    code_map = (lambda t, h: (t, h, 0, 0)) if hc == H_D else (lambda t, h: (t, 0, 0, 0))
    return pl.pallas_call(
        _table_kernel,
        grid=(n, H_D),
        in_specs=[pl.BlockSpec(memory_space=pltpu.SMEM),
                  pl.BlockSpec((1, 1, r, c), code_map)],
        out_specs=pl.BlockSpec((1, 1, r, c), lambda t, h: (t, h, 0, 0)),
        out_shape=jax.ShapeDtypeStruct((n, H_D, r, c), F32),
        compiler_params=_cparams(2),
        name="bias_tables",
    )(vals, codes)


def _inproj_kernel(x_ref, g_ref, w_ref, cos_ref, sin_ref,
                   k_ref, v_ref, kb_ref, vb_ref, q_ref, gd_ref, qr_ref, kr_ref, vr_ref, gr_ref):
    tm = x_ref.shape[0]
    x = x_ref[...]
    inv = lax.rsqrt(jnp.mean(x * x, axis=-1, keepdims=True) + EPS)
    h = (x * inv * g_ref[...]).astype(BF16)

    def proj(i):
        return jnp.dot(h, w_ref[:, _OFF[i]:_OFF[i + 1]], preferred_element_type=F32)

    def store_heads_as_rows(ref, z):
        for hd in range(H_D):
            ref[pl.ds(hd, tm, stride=H_D), :] = z[:, hd * LANES:(hd + 1) * LANES]

    q_ref[...] = (proj(0) * Q_PRESCALE).astype(BF16)
    zk = proj(1)
    store_heads_as_rows(k_ref, zk)
    kb_ref[...] = zk.astype(BF16)
    zv = proj(2)
    store_heads_as_rows(v_ref, zv)
    vb_ref[...] = zv.astype(BF16)
    gd_ref[...] = proj(3).astype(BF16)

    cos = cos_ref[...]
    sin = sin_ref[...]
    lane = lax.broadcasted_iota(jnp.int32, cos.shape, 1)
    low_half = (lane % DK_R) < (DK_R // 2)

    def rotate(z):
        outs = []
        for c in range(z.shape[1] // LANES):
            zc = z[:, c * LANES:(c + 1) * LANES]
            partner = jnp.where(low_half,
                                pltpu.roll(zc, LANES - DK_R // 2, 1),
                                pltpu.roll(zc, DK_R // 2, 1))
            outs.append(zc * cos + partner * sin)
        return jnp.concatenate(outs, axis=1)

    qr_ref[...] = rotate(proj(4)).astype(BF16)
    kr_ref[...] = (rotate(proj(5)) * (DK_R ** -0.5)).astype(BF16)
    vr_ref[...] = proj(6).astype(BF16)
    gr_ref[...] = proj(7).astype(BF16)


def _in_projection(x2d, g_pre, w_bf16, cos_t, sin_t, tm):
    n, d = x2d.shape
    n_tab = cos_t.shape[0] // tm
    row = lambda i: (i, 0)
    tab = lambda i: (i % n_tab, 0)
    outs = [(H_D, 2 * D_QK, F32), (H_D, DV_D, F32), (1, W_D, BF16), (1, W_D, BF16),
            (1, 2 * H_D * D_QK, BF16), (1, W_D, BF16),
            (1, H_R * DK_R, BF16), (1, H_R * DK_R, BF16), (1, W_R, BF16), (1, W_R, BF16)]
    return pl.pallas_call(
        _inproj_kernel,
        grid=(n // tm,),
        in_specs=[pl.BlockSpec((tm, d), row),
                  pl.BlockSpec((1, d), lambda i: (0, 0)),
                  pl.BlockSpec(w_bf16.shape, lambda i: (0, 0)),
                  pl.BlockSpec((tm, LANES), tab),
                  pl.BlockSpec((tm, LANES), tab)],
        out_specs=[pl.BlockSpec((tm * r, w), row) for r, w, _ in outs],
        out_shape=[jax.ShapeDtypeStruct((n * r, w), dt) for r, w, dt in outs],
        compiler_params=_cparams(1),
        name="in_projection",
    )(x2d, g_pre, w_bf16, cos_t, sin_t)


def _attn_kernel(lam_ref, qi_ref, kj_ref, q_ref, kb, v_ref, tab_ref, o_ref, vb, s_bufs, mx_bufs, m_all, acc_all,
                 *, t, n_pairs, unroll):
    n_buf = s_bufs.shape[0]
    assert unroll % n_buf == 0
    s_len = kb.shape[0]
    nq = s_len // t

    @pl.when((pl.program_id(0) == 0) & (pl.program_id(1) == 0))
    def _():
        m_all[...] = jnp.full(m_all.shape, NEG, F32)
        acc_all[...] = jnp.zeros(acc_all.shape, F32)

    vb[:, :DV_D] = v_ref[...]
    vb[:, DV_D:] = jnp.ones((s_len, DV_D), BF16)
    lane = lax.broadcasted_iota(jnp.int32, (t, 2 * D_QK), 1)

    def scores_into(slot, n):
        q = q_ref[pl.ds(pl.multiple_of(qi_ref[n] * t, t), t), :]
        zero = jnp.zeros_like(q)
        qs = jnp.concatenate([jnp.where(lane < D_QK, q, zero), jnp.where(lane >= D_QK, q, zero)], axis=0)
        keys = kb[pl.ds(pl.multiple_of(kj_ref[n] * t, t), t), :]
        tab = tab_ref[jnp.clip(kj_ref[n] - qi_ref[n] + 2, 0, 2), 0]
        s = (lax.dot_general(qs, keys, (((1,), (1,)), ((), ())), preferred_element_type=F32)
             + jnp.concatenate([tab, tab], axis=0))
        s_bufs[slot] = s
        mx = s[:, :LANES]
        for c in range(1, t // LANES):
            mx = jnp.maximum(mx, s[:, c * LANES:(c + 1) * LANES])
        mx_bufs[slot] = mx

    def consume(slot, n):
        i = qi_ref[n]
        j = kj_ref[n]
        m_prev = m_all[i]
        m_new = jnp.maximum(m_prev, jnp.max(mx_bufs[slot], axis=1, keepdims=True))
        corr = jnp.exp2(m_prev - m_new)
        p = jnp.exp2(s_bufs[slot] - jnp.concatenate([m_new] * (t // LANES), axis=1))
        pv = jnp.dot(p.astype(BF16), vb[pl.ds(pl.multiple_of(j * t, t), t), :], preferred_element_type=F32)
        acc_all[i] = acc_all[i] * jnp.concatenate([corr, corr], axis=1) + pv
        m_all[i] = m_new

    scores_into(0, 0)

    def sweep(h, carry):
        n = unroll * h
        for u in range(unroll):
            scores_into((u + 1) % n_buf, n + u + 1)
            consume(u % n_buf, n + u)
        return carry

    lax.fori_loop(0, n_pairs // unroll, sweep, 0)
    for n in range(n_pairs - n_pairs % unroll, n_pairs):
        scores_into((n + 1) % n_buf, n + 1)
        consume(n % n_buf, n)

    for i in range(nq):
        acc = acc_all[i]
        nrm = acc[:, :DV_D] / acc[:, DV_D:]
        o_ref[i * t:(i + 1) * t, :] = (nrm[:t] - lam_ref[0] * nrm[t:]).astype(o_ref.dtype)
        m_all[i] = jnp.full(m_all.shape[1:], NEG, F32)


def _prompt_attention(lam, q2d, k2d, v2d, tabs, batch, seq, t, n_buf=3, unroll=6):
    nq = seq // t
    pairs = [(i, j) for i in range(nq) for j in range(i + 1)]
    pairs.append(pairs[-1])
    qi = jnp.asarray([p[0] for p in pairs], jnp.int32)
    kj = jnp.asarray([p[1] for p in pairs], jnp.int32)
    bh = lambda b, h: (b, h)
    smem = pl.BlockSpec(memory_space=pltpu.SMEM)
    return pl.pallas_call(
        functools.partial(_attn_kernel, t=t, n_pairs=len(pairs) - 1, unroll=unroll),
        grid=(batch, H_D),
        in_specs=[smem, smem, smem,
                  pl.BlockSpec((seq, 2 * D_QK), bh),
                  pl.BlockSpec((seq, 2 * D_QK), bh),
                  pl.BlockSpec((seq, DV_D), bh),
                  pl.BlockSpec((3, 1, t, t), lambda b, h: (0, h, 0, 0))],
        out_specs=pl.BlockSpec((seq, DV_D), bh),
        out_shape=jax.ShapeDtypeStruct((batch * seq, W_D), BF16),
        scratch_shapes=[pltpu.VMEM((seq, 2 * DV_D), BF16),
                        pltpu.VMEM((n_buf, 2 * t, t), F32),
                        pltpu.VMEM((n_buf, 2 * t, LANES), F32),
                        pltpu.VMEM((nq, 2 * t, LANES), F32),
                        pltpu.VMEM((nq, 2 * t, 2 * DV_D), F32)],
        compiler_params=_cparams(2),
        name="prompt_attention",
    )(lam, qi, kj, q2d, k2d, v2d, tabs)


def _ret_kernel(qr_ref, kr_ref, vr_ref, dec_ref, qdec_ref, kdec_ref, o_ref, s_out_ref, st_sc, *, g_chunk, bb):
    c = pl.program_id(1)
    chunk = qr_ref.shape[1]

    @pl.when(c == 0)
    def _():
        st_sc[...] = jnp.zeros(st_sc.shape, F32)

    head_of_lane = lax.broadcasted_iota(jnp.int32, (chunk, H_R * DK_R), 1) // DK_R
    for i in range(bb):
        q = qr_ref[i]
        k = kr_ref[i]
        v = vr_ref[i]
        st = st_sc[i]
        q_all = jnp.concatenate([jnp.where(head_of_lane == h, q, jnp.zeros_like(q)) for h in range(H_R)], axis=0)
        sc_all = lax.dot_general(q_all, k, (((1,), (1,)), ((), ())), preferred_element_type=F32) * dec_ref[...]
        cross_all = jnp.dot(q_all, st.astype(BF16), preferred_element_type=F32) * qdec_ref[...]
        kd_t = (k.astype(F32) * kdec_ref[...]).T
        for h in range(H_R):
            tok = slice(h * chunk, (h + 1) * chunk)
            rows = slice(h * DK_R, (h + 1) * DK_R)
            vh = v[:, h * DV_R:(h + 1) * DV_R]
            inner = jnp.dot(sc_all[tok].astype(BF16), vh, preferred_element_type=F32)
            o_ref[i, :, h * DV_R:(h + 1) * DV_R] = (inner + cross_all[tok]).astype(o_ref.dtype)
            upd = jnp.dot(kd_t[rows, :].astype(BF16), vh, preferred_element_type=F32)
            st_new = g_chunk[h] * st[rows, :] + upd
            st_sc[i, rows, :] = st_new
            s_out_ref[i, h] = st_new


def _prompt_retention(qr, kr, vr, batch, seq, bb=4):
    c = min(RET_CHUNK, seq)
    nc = seq // c
    bb = math.gcd(bb, batch)
    lg = _log_gamma_np()
    idx = np.arange(c, dtype=np.float64)
    d = idx[:, None] - idx[None, :]
    decay = np.where(d[None] >= 0, np.exp(lg[:, None, None] * np.maximum(d, 0.0)[None]), 0.0)
    qdec = np.broadcast_to(np.exp(lg[:, None] * (idx + 1.0)[None, :])[:, :, None], (H_R, c, DV_R))
    kdec = np.repeat(np.exp(lg[:, None] * (c - 1.0 - idx)[None, :]).T, DK_R, axis=1)
    g_chunk = tuple(float(v) for v in np.exp(lg * c))
    blk = lambda w: pl.BlockSpec((bb, c, w), lambda b, j: (b, j, 0))
    fix = lambda b, j: (0, 0)
    as3 = lambda a: a.reshape(batch, seq, a.shape[-1])
    o, s = pl.pallas_call(
        functools.partial(_ret_kernel, g_chunk=g_chunk, bb=bb),
        grid=(batch // bb, nc),
        in_specs=[blk(H_R * DK_R), blk(H_R * DK_R), blk(W_R),
                  pl.BlockSpec((H_R * c, c), fix),
                  pl.BlockSpec((H_R * c, DV_R), fix),
                  pl.BlockSpec((c, H_R * DK_R), fix)],
        out_specs=[blk(W_R),
                   pl.BlockSpec((bb, H_R, DK_R, DV_R), lambda b, j: (b, 0, 0, 0))],
        out_shape=[jax.ShapeDtypeStruct((batch, seq, W_R), BF16),
                   jax.ShapeDtypeStruct((batch, H_R, DK_R, DV_R), F32)],
        scratch_shapes=[pltpu.VMEM((bb, H_R * DK_R, DV_R), F32)],
        compiler_params=_cparams(2),
        name="prompt_retention",
    )(as3(qr), as3(kr), as3(vr), jnp.asarray(decay.reshape(H_R * c, c), F32),
      jnp.asarray(qdec.reshape(H_R * c, DV_R), F32), jnp.asarray(kdec, F32))
    return o.reshape(batch * seq, W_R), s


def _decode_kernel(pt_ref, lam_ref, q_ref, kn_ref, vn_ref, tab_ref, ck_hbm, cv_hbm, o_ref,
                   kbuf, vbuf, knew, vnew, m_sc, l_sc, acc_sc, sem, *, n_batch, n_pages, group, n_slot, t_new):
    n_group = n_pages // group
    rows = H_D * 2 * t_new
    width = kbuf.shape[2]

    def page_copies(b, g, slot, real):
        out = []
        for p in range(group):
            phys = pt_ref[b * n_pages + g * group + p] if real else 0
            out.append(pltpu.make_async_copy(ck_hbm.at[phys], kbuf.at[slot, p], sem.at[0, slot]))
            out.append(pltpu.make_async_copy(cv_hbm.at[phys], vbuf.at[slot, p], sem.at[1, slot]))
        return out

    def start_group(b, g, slot):
        for cp in page_copies(b, g, slot, True):
            cp.start()

    b = pl.program_id(0)
    g = pl.program_id(1)

    @pl.when((b == 0) & (g == 0))
    def _():
        for g0 in range(min(n_slot, n_group)):
            start_group(0, g0, g0 % n_slot)
        knew[...] = jnp.zeros(knew.shape, F32)
        vnew[...] = jnp.zeros(vnew.shape, F32)

    row = lax.broadcasted_iota(jnp.int32, (rows, 2 * D_QK), 0)
    lane = lax.broadcasted_iota(jnp.int32, (rows, 2 * D_QK), 1)
    own_half = ((row // t_new) % 2 == 0) == (lane < D_QK)
    q = jnp.where(own_half, q_ref[0], 0.0).astype(BF16)

    def softmax_update(state, scores, values):
        m, l, acc = state
        mx = scores[0]
        for s in scores[1:]:
            mx = jnp.maximum(mx, s)
        m_new = jnp.maximum(m, jnp.broadcast_to(jnp.max(mx, axis=1, keepdims=True), m.shape))
        corr = jnp.exp2(m - m_new)
        m_wide = jnp.concatenate([m_new] * (width // LANES), axis=1)
        ps = [jnp.exp2(s - m_wide) for s in scores]
        psum = ps[0]
        for p in ps[1:]:
            psum = psum + p
        l_new = l * corr + jnp.broadcast_to(jnp.sum(psum, axis=1, keepdims=True), l.shape)
        pv = jnp.dot(ps[0].astype(BF16), values[0], preferred_element_type=F32)
        for p, vv in zip(ps[1:], values[1:]):
            pv = pv + jnp.dot(p.astype(BF16), vv, preferred_element_type=F32)
        return m_new, l_new, acc * corr + pv

    def qk(keys):
        return lax.dot_general(q, keys, (((1,), (1,)), ((), ())), preferred_element_type=F32)

    @pl.when(g == 0)
    def _():
        m_sc[...] = jnp.full(m_sc.shape, NEG, F32)
        l_sc[...] = jnp.zeros(l_sc.shape, F32)
        acc_sc[...] = jnp.zeros(acc_sc.shape, F32)

    slot = lax.rem(g, n_slot)
    for cp in page_copies(b, g, slot, False):
        cp.wait()
    scores = [qk(kbuf[slot, p].astype(BF16)) + tab_ref[0] for p in range(group - 1)]
    scores.append(qk(kbuf[slot, group - 1].astype(BF16)) + tab_ref[jnp.where(g == n_group - 1, 1, 0)])
    values = [vbuf[slot, p].astype(BF16) for p in range(group)]
    m_sc[...], l_sc[...], acc_sc[...] = softmax_update((m_sc[...], l_sc[...], acc_sc[...]), scores, values)

    nxt = b * n_group + g + n_slot

    @pl.when(nxt < n_batch * n_group)
    def _():
        start_group(nxt // n_group, lax.rem(nxt, n_group), slot)

    @pl.when(g == n_group - 1)
    def _():
        n_new = kn_ref.shape[1]
        knew[0:n_new, :] = kn_ref[0]
        vnew[0:n_new, :] = vn_ref[0]
        s_new = qk(knew[...].astype(BF16)) + tab_ref[2]
        _, l, acc = softmax_update((m_sc[...], l_sc[...], acc_sc[...]), [s_new], [vnew[...].astype(BF16)])
        n = acc / l
        o_ref[0] = n - lam_ref[0] * pltpu.roll(n, rows - t_new, 0)


def _decode_attention(page_table, lam, q_rows, k_new, v_new, tabs, cache_k, cache_v, group=8, n_slot=4):
    n_batch, rows, _ = q_rows.shape
    n_pages = page_table.shape[1]
    width = cache_k.shape[1]
    group = math.gcd(group, n_pages)
    n_slot = max(1, math.gcd(n_slot, n_pages // group))
    per_b = lambda a: pl.BlockSpec((1,) + a.shape[1:], lambda b, g: (b, 0, 0))
    return pl.pallas_call(
        functools.partial(_decode_kernel, n_batch=n_batch, n_pages=n_pages, group=group,
                          n_slot=n_slot, t_new=rows // (2 * H_D)),
        grid=(n_batch, n_pages // group),
        in_specs=[pl.BlockSpec(memory_space=pltpu.SMEM),
                  pl.BlockSpec(memory_space=pltpu.SMEM),
                  per_b(q_rows), per_b(k_new), per_b(v_new),
                  pl.BlockSpec(tabs.shape, lambda b, g: (0, 0, 0)),
                  pl.BlockSpec(memory_space=pl.ANY),
                  pl.BlockSpec(memory_space=pl.ANY)],
        out_specs=per_b(q_rows),
        out_shape=jax.ShapeDtypeStruct(q_rows.shape, F32),
        scratch_shapes=[pltpu.VMEM((n_slot, group, width, LANES), F32),
                        pltpu.VMEM((n_slot, group, width, LANES), F32),
                        pltpu.VMEM((width, LANES), F32),
                        pltpu.VMEM((width, LANES), F32),
                        pltpu.VMEM((rows, LANES), F32),
                        pltpu.VMEM((rows, LANES), F32),
                        pltpu.VMEM((rows, DV_D), F32),
                        pltpu.SemaphoreType.DMA((2, n_slot))],
        compiler_params=_cparams(2),
        name="decode_attention",
    )(page_table.reshape(-1), lam, q_rows, k_new, v_new, tabs, cache_k, cache_v)


def _sample_ret_kernel(qt_ref, kt_ref, v_ref, s_ref, o_ref, s_out_ref, *, gamma, bb, t_new):
    for i in range(bb):
        v = v_ref[i]
        for h in range(H_R):
            st = s_ref[i, h]
            qt = qt_ref[i, h]
            kt = kt_ref[i, h]
            outs = []
            for t in range(t_new):
                vrow = v[t:t + 1, h * DV_R:(h + 1) * DV_R]
                st = gamma[h] * st + kt[:, t:t + 1] * vrow
                outs.append(jnp.sum(qt[:, t:t + 1] * st, axis=0, keepdims=True))
            o_ref[i, :, h * DV_R:(h + 1) * DV_R] = jnp.concatenate(outs, axis=0)
            s_out_ref[i, h] = st


def _sample_retention(qr_t, kr_t, vr, state, bb=8):
    n_batch, _, _, t_new = qr_t.shape
    bb = math.gcd(bb, n_batch)
    gamma = tuple(float(v) for v in np.exp(_log_gamma_np()))
    m4 = lambda i: (i, 0, 0, 0)
    m3 = lambda i: (i, 0, 0)
    return pl.pallas_call(
        functools.partial(_sample_ret_kernel, gamma=gamma, bb=bb, t_new=t_new),
        grid=(n_batch // bb,),
        in_specs=[pl.BlockSpec((bb, H_R, DK_R, t_new), m4),
                  pl.BlockSpec((bb, H_R, DK_R, t_new), m4),
                  pl.BlockSpec((bb, t_new, W_R), m3),
                  pl.BlockSpec((bb, H_R, DK_R, DV_R), m4)],
        out_specs=[pl.BlockSpec((bb, t_new, W_R), m3),
                   pl.BlockSpec((bb, H_R, DK_R, DV_R), m4)],
        out_shape=[jax.ShapeDtypeStruct((n_batch, t_new, W_R), F32),
                   jax.ShapeDtypeStruct(state.shape, F32)],
        compiler_params=_cparams(1),
        name="sample_retention",
    )(qr_t, kr_t, vr, state)


def _merge_kernel(x_ref, od_ref, or_ref, gd_ref, gr_ref, gnd_ref, gnr_ref, w_ref, gpost_ref, y_ref):
    def gated(o_ref_, g_ref_, gn):
        parts = []
        for h in range(o_ref_.shape[1] // LANES):
            cols = slice(h * LANES, (h + 1) * LANES)
            o = o_ref_[:, cols].astype(F32)
            g = g_ref_[:, cols].astype(F32)
            y = o * lax.rsqrt(jnp.mean(o * o, axis=-1, keepdims=True) + EPS) * gn
            parts.append((y * (g * jax.nn.sigmoid(g))).astype(BF16))
        return parts

    o = jnp.concatenate(gated(od_ref, gd_ref, gnd_ref[...]) + gated(or_ref, gr_ref, gnr_ref[...]), axis=1)
    mix = jnp.dot(o, w_ref[...], preferred_element_type=F32)
    y = mix * lax.rsqrt(jnp.mean(mix * mix, axis=-1, keepdims=True) + EPS) * gpost_ref[...]
    y_ref[...] = x_ref[...] + y


def _merge(x2d, od, orr, gd, gr, gn_d, gn_r, w_bf16, g_post, tm):
    n, d = x2d.shape
    row = lambda i: (i, 0)
    fix = lambda i: (0, 0)
    return pl.pallas_call(
        _merge_kernel,
        grid=(n // tm,),
        in_specs=[pl.BlockSpec((tm, d), row),
                  pl.BlockSpec((tm, W_D), row),
                  pl.BlockSpec((tm, W_R), row),
                  pl.BlockSpec((tm, W_D), row),
                  pl.BlockSpec((tm, W_R), row),
                  pl.BlockSpec((1, DV_D), fix),
                  pl.BlockSpec((1, DV_R), fix),
                  pl.BlockSpec(w_bf16.shape, fix),
                  pl.BlockSpec((1, d), fix)],
        out_specs=pl.BlockSpec((tm, d), row),
        out_shape=jax.ShapeDtypeStruct((n, d), F32),
        compiler_params=_cparams(1),
        name="merge_out_projection",
    )(x2d, od, orr, gd, gr, gn_d, gn_r, w_bf16, g_post)


def _pick_tile(n, target):
    t = math.gcd(n, target)
    assert t % 8 == 0, (n, target)
    return t


def kernel(x_prompt, x_sample, cache_k, cache_v, state_ret, page_table, rel_bias, norm_pre, norm_post,
           w_in, w_out, lambda_q1, lambda_k1, lambda_q2, lambda_k2, gn_diff, gn_ret):
    depth = w_in.shape[0]
    batch, seq, d_model = x_prompt.shape
    dec_batch, t_new, _ = x_sample.shape
    n_pool, page = cache_k.shape[1], cache_k.shape[2]
    n_pages = page_table.shape[1]
    past = n_pages * page
    far = _far_distance()

    t_attn = _pick_tile(seq, 512)
    assert t_attn > far and page > far, "bias must be constant beyond the neighbouring tile / page"
    tm_p = _pick_tile(batch * seq, 512)
    assert seq % tm_p == 0
    tm_s = _pick_tile(dec_batch * t_new, 512)

    vals = jnp.concatenate([(rel_bias - rel_bias[NUM_BUCKETS - 1:]) * LOG2E,
                            jnp.full((1, H_D), NEG, F32)], axis=0).astype(F32)
    tabs_p = _bias_tables(vals, jnp.asarray(_prompt_bucket_codes(t_attn)))
    tabs_s = _bias_tables(vals, jnp.asarray(_decode_bucket_codes(page, t_new)))

    cos_p, sin_p = _rotation_tables(jnp.arange(seq, dtype=jnp.int32))
    pos_s = past + (jnp.arange(tm_s, dtype=jnp.int32) % t_new)
    cos_s, sin_s = _rotation_tables(pos_s)

    yp = x_prompt.reshape(batch * seq, d_model)
    ys = x_sample.reshape(dec_batch * t_new, d_model)
    kp_l, vp_l, sp_l, ks_l, vs_l, ss_l = [], [], [], [], [], []
    for layer in range(depth):
        lambda_init = 0.8 - 0.6 * math.exp(-0.3 * layer)
        lam = (jnp.exp(jnp.sum(lambda_q1[layer].astype(F32) * lambda_k1[layer].astype(F32)))
               - jnp.exp(jnp.sum(lambda_q2[layer].astype(F32) * lambda_k2[layer].astype(F32)))
               + lambda_init).reshape(1).astype(F32)
        w_in_b = w_in[layer].astype(BF16)
        w_out_b = w_out[layer].astype(BF16)
        g_pre = norm_pre[layer].reshape(1, d_model)
        g_post = norm_post[layer].reshape(1, d_model)
        gn_d = (gn_diff[layer].astype(F32) * (1.0 - lambda_init)).reshape(1, DV_D)
        gn_r = gn_ret[layer].astype(F32).reshape(1, DV_R)

        k_p, v_p, kb_p, vb_p, q_p, gd_p, qr_p, kr_p, vr_p, gr_p = _in_projection(
            yp, g_pre, w_in_b, cos_p, sin_p, tm_p)
        od_p = _prompt_attention(lam, q_p, kb_p, vb_p, tabs_p, batch, seq, t_attn)
        or_p, s_p = _prompt_retention(qr_p, kr_p, vr_p, batch, seq)
        yp = _merge(yp, od_p, or_p, gd_p, gr_p, gn_d, gn_r, w_out_b, g_post, tm_p)
        kp_l.append(k_p.reshape(batch, seq, H_D, 2 * D_QK))
        vp_l.append(v_p.reshape(batch, seq, H_D, DV_D))
        sp_l.append(s_p)

        k_s, v_s, _, _, q_s, gd_s, qr_s, kr_s, vr_s, gr_s = _in_projection(
            ys, g_pre, w_in_b, cos_s, sin_s, tm_s)
        q4 = q_s.astype(F32).reshape(dec_batch, t_new, H_D, 2 * D_QK).transpose(0, 2, 1, 3)
        q_rows = jnp.concatenate([q4, q4], axis=2).reshape(dec_batch, H_D * 2 * t_new, 2 * D_QK)
        od_s = _decode_attention(page_table + layer * n_pool, lam, q_rows,
                                 k_s.reshape(dec_batch, t_new * H_D, 2 * D_QK),
                                 v_s.reshape(dec_batch, t_new * H_D, DV_D),
                                 tabs_s.reshape(3, H_D * 2 * t_new, page * H_D),
                                 cache_k.reshape(depth * n_pool, page * H_D, 2 * D_QK),
                                 cache_v.reshape(depth * n_pool, page * H_D, DV_D))
        od_s = od_s.reshape(dec_batch, H_D, 2, t_new, DV_D)[:, :, 0].transpose(0, 2, 1, 3)
        od_s = od_s.reshape(dec_batch * t_new, W_D)
        to_cols = lambda a: a.astype(F32).reshape(dec_batch, t_new, H_R, DK_R).transpose(0, 2, 3, 1)
        or_s, s_s = _sample_retention(to_cols(qr_s), to_cols(kr_s),
                                      vr_s.astype(F32).reshape(dec_batch, t_new, W_R),
                                      state_ret[layer].astype(F32))
        ys = _merge(ys, od_s, or_s.reshape(dec_batch * t_new, W_R), gd_s, gr_s, gn_d, gn_r, w_out_b, g_post, tm_s)
        ks_l.append(k_s.reshape(dec_batch, t_new, H_D, 2 * D_QK))
        vs_l.append(v_s.reshape(dec_batch, t_new, H_D, DV_D))
        ss_l.append(s_s)

    stack = lambda parts: parts[0][None] if len(parts) == 1 else jnp.stack(parts)
    return (yp.reshape(batch, seq, d_model), ys.reshape(dec_batch, t_new, d_model),
            stack(kp_l), stack(vp_l), stack(sp_l), stack(ks_l), stack(vs_l), stack(ss_l))
```

```python
import functools
import math

import jax
import jax.numpy as jnp
import numpy as np
from jax import lax
from jax.experimental import pallas as pl
from jax.experimental.pallas import tpu as pltpu

F32 = jnp.float32
BF16 = jnp.bfloat16

H_D = 4
D_QK = 64
DV_D = 128
W_D = H_D * DV_D
H_R = 4
DK_R = 64
DV_R = 128
W_R = H_R * DV_R
NUM_BUCKETS = 32
MAX_DISTANCE = 128
RET_CHUNK = 128
EPS = 1e-6
NEG = -1e30
LOG2E = 1.4426950408889634
Q_PRESCALE = D_QK ** -0.5 * LOG2E
MASK_CODE = NUM_BUCKETS
LANES = 128
VMEM_LIMIT = 56 * 1024 * 1024

_SIZES = [H_D * 2 * D_QK, H_D * 2 * D_QK, W_D, W_D, H_R * DK_R, H_R * DK_R, W_R, W_R]
_OFF = [0] + [int(v) for v in np.cumsum(_SIZES)]


def _cparams(n_axes):
    return pltpu.CompilerParams(dimension_semantics=("arbitrary",) * n_axes,
                                vmem_limit_bytes=VMEM_LIMIT)


def _bucket_np(rel):
    n = np.maximum(rel, 0)
    max_exact = NUM_BUCKETS // 2
    nf = np.maximum(n, max_exact).astype(np.float32)
    large = max_exact + (np.log(nf / np.float32(max_exact)) / np.float32(math.log(MAX_DISTANCE / max_exact))
                         * np.float32(NUM_BUCKETS - max_exact)).astype(np.int32)
    large = np.minimum(large, NUM_BUCKETS - 1)
    return np.where(n < max_exact, n, large).astype(np.int32)


def _far_distance():
    b = _bucket_np(np.arange(4 * MAX_DISTANCE))
    return int(np.max(np.nonzero(b != NUM_BUCKETS - 1)[0])) + 1


def _prompt_bucket_codes(t):
    r = np.arange(t)[:, None]
    c = np.arange(t)[None, :]
    far = np.full((t, t), NUM_BUCKETS - 1)
    left = _bucket_np(r - c + t)
    diag = np.where(r >= c, _bucket_np(r - c), MASK_CODE)
    return np.stack([far, left, diag]).astype(np.int32)[:, None]


def _decode_bucket_codes(page, t_new):
    h = np.arange(H_D)[:, None, None]
    t = np.tile(np.arange(t_new), 2)[None, :, None]
    col = np.arange(page * H_D)[None, None, :]
    j, hk = col // H_D, col % H_D
    own = hk == h
    far = np.where(own, NUM_BUCKETS - 1, MASK_CODE) + 0 * t
    last = np.where(own, _bucket_np(page + t - j), MASK_CODE)
    new = np.where(own & (j <= t) & (j < t_new), _bucket_np(t - j), MASK_CODE)
    return np.stack([far, last, new]).astype(np.int32)


def _rotation_tables(pos):
    half = DK_R // 2
    inv = 1.0 / (10000.0 ** jnp.linspace(0.0, 1.0, half, dtype=F32))
    ang = pos.astype(F32)[:, None] * inv[None, :]
    cos = jnp.cos(ang)
    sin = jnp.sin(ang)
    cos_t = jnp.tile(cos, (1, LANES // half))
    sin_t = jnp.tile(jnp.concatenate([-sin, sin], axis=1), (1, LANES // DK_R))
    return cos_t, sin_t


def _log_gamma_np():
    return np.log(1.0 - 2.0 ** (-5.0 - np.arange(H_R, dtype=np.float64)))


def _table_kernel(vals_ref, code_ref, out_ref):
    h = pl.program_id(1)
    code = code_ref[0, 0]
    acc = jnp.full(code.shape, vals_ref[MASK_CODE, h], F32)
    for b in range(NUM_BUCKETS):
        acc = jnp.where(code == b, vals_ref[b, h], acc)
    out_ref[0, 0] = acc


def _bias_tables(vals, codes):
    n, hc, r, c = codes.shape
    code_map = (lambda t, h: (t, h, 0, 0)) if hc == H_D else (lambda t, h: (t, 0, 0, 0))
    return pl.pallas_call(
        _table_kernel,
        grid=(n, H_D),
        in_specs=[pl.BlockSpec(memory_space=pltpu.SMEM),
                  pl.BlockSpec((1, 1, r, c), code_map)],
        out_specs=pl.BlockSpec((1, 1, r, c), lambda t, h: (t, h, 0, 0)),
        out_shape=jax.ShapeDtypeStruct((n, H_D, r, c), F32),
        compiler_params=_cparams(2),
        name="bias_tables",
    )(vals, codes)


_N_INPROJ_IN = 5
_N_INPROJ_OUT = 10


def _inproj_kernel(x_ref, g_ref, w_ref, cos_ref, sin_ref,
                   k_ref, v_ref, kb_ref, vb_ref, q_ref, gd_ref, qr_ref, kr_ref, vr_ref, gr_ref):
    tm = x_ref.shape[0]
    x = x_ref[...]
    inv = lax.rsqrt(jnp.mean(x * x, axis=-1, keepdims=True) + EPS)
    h = (x * inv * g_ref[...]).astype(BF16)

    def proj(i):
        return jnp.dot(h, w_ref[:, _OFF[i]:_OFF[i + 1]], preferred_element_type=F32)

    def store_heads_as_rows(ref, z):
        for hd in range(H_D):
            ref[pl.ds(hd, tm, stride=H_D), :] = z[:, hd * LANES:(hd + 1) * LANES]

    q_ref[...] = (proj(0) * Q_PRESCALE).astype(BF16)
    zk = proj(1)
    store_heads_as_rows(k_ref, zk)
    kb_ref[...] = zk.astype(BF16)
    zv = proj(2)
    store_heads_as_rows(v_ref, zv)
    vb_ref[...] = zv.astype(BF16)
    gd_ref[...] = proj(3).astype(BF16)

    cos = cos_ref[...]
    sin = sin_ref[...]
    lane = lax.broadcasted_iota(jnp.int32, cos.shape, 1)
    low_half = (lane % DK_R) < (DK_R // 2)

    def rotate(z):
        outs = []
        for c in range(z.shape[1] // LANES):
            zc = z[:, c * LANES:(c + 1) * LANES]
            partner = jnp.where(low_half,
                                pltpu.roll(zc, LANES - DK_R // 2, 1),
                                pltpu.roll(zc, DK_R // 2, 1))
            outs.append(zc * cos + partner * sin)
        return jnp.concatenate(outs, axis=1)

    qr_ref[...] = rotate(proj(4)).astype(BF16)
    kr_ref[...] = (rotate(proj(5)) * (DK_R ** -0.5)).astype(BF16)
    vr_ref[...] = proj(6).astype(BF16)
    gr_ref[...] = proj(7).astype(BF16)


def _inproj_specs(x2d, w_bf16, cos_t, tm):
    n, d = x2d.shape
    n_tab = cos_t.shape[0] // tm
    row = lambda i: (i, 0)
    tab = lambda i: (i % n_tab, 0)
    outs = [(H_D, 2 * D_QK, F32), (H_D, DV_D, F32), (1, W_D, BF16), (1, W_D, BF16),
            (1, 2 * H_D * D_QK, BF16), (1, W_D, BF16),
            (1, H_R * DK_R, BF16), (1, H_R * DK_R, BF16), (1, W_R, BF16), (1, W_R, BF16)]
    in_specs = [pl.BlockSpec((tm, d), row),
                pl.BlockSpec((1, d), lambda i: (0, 0)),
                pl.BlockSpec(w_bf16.shape, lambda i: (0, 0)),
                pl.BlockSpec((tm, LANES), tab),
                pl.BlockSpec((tm, LANES), tab)]
    out_specs = [pl.BlockSpec((tm * r, w), row) for r, w, _ in outs]
    out_shape = [jax.ShapeDtypeStruct((n * r, w), dt) for r, w, dt in outs]
    return in_specs, out_specs, out_shape


def _in_projection(x2d, g_pre, w_bf16, cos_t, sin_t, tm):
    in_specs, out_specs, out_shape = _inproj_specs(x2d, w_bf16, cos_t, tm)
    return pl.pallas_call(
        _inproj_kernel,
        grid=(x2d.shape[0] // tm,),
        in_specs=in_specs,
        out_specs=out_specs,
        out_shape=out_shape,
        compiler_params=_cparams(1),
        name="in_projection",
    )(x2d, g_pre, w_bf16, cos_t, sin_t)


def _attn_kernel(lam_ref, qi_ref, kj_ref, q_ref, kb, v_ref, tab_ref, o_ref, vb, s_bufs, mx_bufs, m_all, acc_all,
                 *, t, n_pairs, unroll):
    n_buf = s_bufs.shape[0]
    assert unroll % n_buf == 0
    s_len = kb.shape[0]
    nq = s_len // t

    @pl.when((pl.program_id(0) == 0) & (pl.program_id(1) == 0))
    def _():
        m_all[...] = jnp.full(m_all.shape, NEG, F32)
        acc_all[...] = jnp.zeros(acc_all.shape, F32)

    vb[:, :DV_D] = v_ref[...]
    vb[:, DV_D:] = jnp.ones((s_len, DV_D), BF16)
    lane = lax.broadcasted_iota(jnp.int32, (t, 2 * D_QK), 1)

    def scores_into(slot, n):
        q = q_ref[pl.ds(pl.multiple_of(qi_ref[n] * t, t), t), :]
        zero = jnp.zeros_like(q)
        qs = jnp.concatenate([jnp.where(lane < D_QK, q, zero), jnp.where(lane >= D_QK, q, zero)], axis=0)
        keys = kb[pl.ds(pl.multiple_of(kj_ref[n] * t, t), t), :]
        tab = tab_ref[jnp.clip(kj_ref[n] - qi_ref[n] + 2, 0, 2), 0]
        s = (lax.dot_general(qs, keys, (((1,), (1,)), ((), ())), preferred_element_type=F32)
             + jnp.concatenate([tab, tab], axis=0))
        s_bufs[slot] = s
        mx = s[:, :LANES]
        for c in range(1, t // LANES):
            mx = jnp.maximum(mx, s[:, c * LANES:(c + 1) * LANES])
        mx_bufs[slot] = mx

    def consume(slot, n):
        i = qi_ref[n]
        j = kj_ref[n]
        m_prev = m_all[i]
        m_new = jnp.maximum(m_prev, jnp.max(mx_bufs[slot], axis=1, keepdims=True))
        corr = jnp.exp2(m_prev - m_new)
        p = jnp.exp2(s_bufs[slot] - jnp.concatenate([m_new] * (t // LANES), axis=1))
        pv = jnp.dot(p.astype(BF16), vb[pl.ds(pl.multiple_of(j * t, t), t), :], preferred_element_type=F32)
        acc_all[i] = acc_all[i] * jnp.concatenate([corr, corr], axis=1) + pv
        m_all[i] = m_new

    scores_into(0, 0)

    def sweep(h, carry):
        n = unroll * h
        for u in range(unroll):
            scores_into((u + 1) % n_buf, n + u + 1)
            consume(u % n_buf, n + u)
        return carry

    lax.fori_loop(0, n_pairs // unroll, sweep, 0)
    for n in range(n_pairs - n_pairs % unroll, n_pairs):
        scores_into((n + 1) % n_buf, n + 1)
        consume(n % n_buf, n)

    for i in range(nq):
        acc = acc_all[i]
        nrm = acc[:, :DV_D] / acc[:, DV_D:]
        o_ref[i * t:(i + 1) * t, :] = (nrm[:t] - lam_ref[0] * nrm[t:]).astype(o_ref.dtype)
        m_all[i] = jnp.full(m_all.shape[1:], NEG, F32)


def _prompt_attention(lam, q2d, k2d, v2d, tabs, batch, seq, t, n_buf=3, unroll=6):
    nq = seq // t
    pairs = [(i, j) for i in range(nq) for j in range(i + 1)]
    pairs.append(pairs[-1])
    qi = jnp.asarray([p[0] for p in pairs], jnp.int32)
    kj = jnp.asarray([p[1] for p in pairs], jnp.int32)
    bh = lambda b, h: (b, h)
    smem = pl.BlockSpec(memory_space=pltpu.SMEM)
    return pl.pallas_call(
        functools.partial(_attn_kernel, t=t, n_pairs=len(pairs) - 1, unroll=unroll),
        grid=(batch, H_D),
        in_specs=[smem, smem, smem,
                  pl.BlockSpec((seq, 2 * D_QK), bh),
                  pl.BlockSpec((seq, 2 * D_QK), bh),
                  pl.BlockSpec((seq, DV_D), bh),
                  pl.BlockSpec((3, 1, t, t), lambda b, h: (0, h, 0, 0))],
        out_specs=pl.BlockSpec((seq, DV_D), bh),
        out_shape=jax.ShapeDtypeStruct((batch * seq, W_D), BF16),
        scratch_shapes=[pltpu.VMEM((seq, 2 * DV_D), BF16),
                        pltpu.VMEM((n_buf, 2 * t, t), F32),
                        pltpu.VMEM((n_buf, 2 * t, LANES), F32),
                        pltpu.VMEM((nq, 2 * t, LANES), F32),
                        pltpu.VMEM((nq, 2 * t, 2 * DV_D), F32)],
        compiler_params=_cparams(2),
        name="prompt_attention",
    )(lam, qi, kj, q2d, k2d, v2d, tabs)


def _ret_kernel(qr_ref, kr_ref, vr_ref, dec_ref, qdec_ref, kdec_ref, o_ref, s_out_ref, st_sc, *, g_chunk, bb):
    c = pl.program_id(1)
    chunk = qr_ref.shape[1]

    @pl.when(c == 0)
    def _():
        st_sc[...] = jnp.zeros(st_sc.shape, F32)

    head_of_lane = lax.broadcasted_iota(jnp.int32, (chunk, H_R * DK_R), 1) // DK_R
    for i in range(bb):
        q = qr_ref[i]
        k = kr_ref[i]
        v = vr_ref[i]
        st = st_sc[i]
        q_all = jnp.concatenate([jnp.where(head_of_lane == h, q, jnp.zeros_like(q)) for h in range(H_R)], axis=0)
        sc_all = lax.dot_general(q_all, k, (((1,), (1,)), ((), ())), preferred_element_type=F32) * dec_ref[...]
        cross_all = jnp.dot(q_all, st.astype(BF16), preferred_element_type=F32) * qdec_ref[...]
        kd_t = (k.astype(F32) * kdec_ref[...]).T
        for h in range(H_R):
            tok = slice(h * chunk, (h + 1) * chunk)
            rows = slice(h * DK_R, (h + 1) * DK_R)
            vh = v[:, h * DV_R:(h + 1) * DV_R]
            inner = jnp.dot(sc_all[tok].astype(BF16), vh, preferred_element_type=F32)
            o_ref[i, :, h * DV_R:(h + 1) * DV_R] = (inner + cross_all[tok]).astype(o_ref.dtype)
            upd = jnp.dot(kd_t[rows, :].astype(BF16), vh, preferred_element_type=F32)
            st_new = g_chunk[h] * st[rows, :] + upd
            st_sc[i, rows, :] = st_new
            s_out_ref[i, h] = st_new


def _prompt_retention(qr, kr, vr, batch, seq, bb=4):
    c = min(RET_CHUNK, seq)
    nc = seq // c
    bb = math.gcd(bb, batch)
    lg = _log_gamma_np()
    idx = np.arange(c, dtype=np.float64)
    d = idx[:, None] - idx[None, :]
    decay = np.where(d[None] >= 0, np.exp(lg[:, None, None] * np.maximum(d, 0.0)[None]), 0.0)
    qdec = np.broadcast_to(np.exp(lg[:, None] * (idx + 1.0)[None, :])[:, :, None], (H_R, c, DV_R))
    kdec = np.repeat(np.exp(lg[:, None] * (c - 1.0 - idx)[None, :]).T, DK_R, axis=1)
    g_chunk = tuple(float(v) for v in np.exp(lg * c))
    blk = lambda w: pl.BlockSpec((bb, c, w), lambda b, j: (b, j, 0))
    fix = lambda b, j: (0, 0)
    as3 = lambda a: a.reshape(batch, seq, a.shape[-1])
    o, s = pl.pallas_call(
        functools.partial(_ret_kernel, g_chunk=g_chunk, bb=bb),
        grid=(batch // bb, nc),
        in_specs=[blk(H_R * DK_R), blk(H_R * DK_R), blk(W_R),
                  pl.BlockSpec((H_R * c, c), fix),
                  pl.BlockSpec((H_R * c, DV_R), fix),
                  pl.BlockSpec((c, H_R * DK_R), fix)],
        out_specs=[blk(W_R),
                   pl.BlockSpec((bb, H_R, DK_R, DV_R), lambda b, j: (b, 0, 0, 0))],
        out_shape=[jax.ShapeDtypeStruct((batch, seq, W_R), BF16),
                   jax.ShapeDtypeStruct((batch, H_R, DK_R, DV_R), F32)],
        scratch_shapes=[pltpu.VMEM((bb, H_R * DK_R, DV_R), F32)],
        compiler_params=_cparams(2),
        name="prompt_retention",
    )(as3(qr), as3(kr), as3(vr), jnp.asarray(decay.reshape(H_R * c, c), F32),
      jnp.asarray(qdec.reshape(H_R * c, DV_R), F32), jnp.asarray(kdec, F32))
    return o.reshape(batch * seq, W_R), s


def _decode_row(b, r, pt_ref, lam_ref, q_ref, kn_ref, vn_ref, tab_ref, ck_hbm, cv_hbm, o_ref,
                kbuf, vbuf, knew, vnew, sem, *, n_batch, n_pages, group, n_slot, t_new):
    n_group = n_pages // group
    rows = H_D * 2 * t_new
    width = kbuf.shape[2]

    def page_copies(bb, g, slot, real):
        out = []
        for p in range(group):
            phys = pt_ref[bb * n_pages + g * group + p] if real else 0
            out.append(pltpu.make_async_copy(ck_hbm.at[phys], kbuf.at[slot, p], sem.at[0, slot]))
            out.append(pltpu.make_async_copy(cv_hbm.at[phys], vbuf.at[slot, p], sem.at[1, slot]))
        return out

    def start_group(bb, g, slot):
        for cp in page_copies(bb, g, slot, True):
            cp.start()

    @pl.when(b == 0)
    def _():
        for g0 in range(min(n_slot, n_group)):
            start_group(0, g0, g0 % n_slot)
        knew[...] = jnp.zeros(knew.shape, F32)
        vnew[...] = jnp.zeros(vnew.shape, F32)

    row = lax.broadcasted_iota(jnp.int32, (rows, 2 * D_QK), 0)
    lane = lax.broadcasted_iota(jnp.int32, (rows, 2 * D_QK), 1)
    own_half = ((row // t_new) % 2 == 0) == (lane < D_QK)
    q = jnp.where(own_half, q_ref[r], 0.0).astype(BF16)

    def softmax_update(state, scores, values):
        m, l, acc = state
        mx = scores[0]
        for s in scores[1:]:
            mx = jnp.maximum(mx, s)
        m_new = jnp.maximum(m, jnp.broadcast_to(jnp.max(mx, axis=1, keepdims=True), m.shape))
        corr = jnp.exp2(m - m_new)
        m_wide = jnp.concatenate([m_new] * (width // LANES), axis=1)
        ps = [jnp.exp2(s - m_wide) for s in scores]
        psum = ps[0]
        for p in ps[1:]:
            psum = psum + p
        l_new = l * corr + jnp.broadcast_to(jnp.sum(psum, axis=1, keepdims=True), l.shape)
        pv = jnp.dot(ps[0].astype(BF16), values[0], preferred_element_type=F32)
        for p, vv in zip(ps[1:], values[1:]):
            pv = pv + jnp.dot(p.astype(BF16), vv, preferred_element_type=F32)
        return m_new, l_new, acc * corr + pv

    def qk(keys):
        return lax.dot_general(q, keys, (((1,), (1,)), ((), ())), preferred_element_type=F32)

    state = (jnp.full((rows, LANES), NEG, F32), jnp.zeros((rows, LANES), F32), jnp.zeros((rows, DV_D), F32))
    for g in range(n_group):
        slot = g % n_slot
        for cp in page_copies(b, g, slot, False):
            cp.wait()
        tabs = [tab_ref[0]] * (group - 1) + [tab_ref[1 if g == n_group - 1 else 0]]
        scores = [qk(kbuf[slot, p].astype(BF16)) + tabs[p] for p in range(group)]
        values = [vbuf[slot, p].astype(BF16) for p in range(group)]
        state = softmax_update(state, scores, values)
        if g + n_slot < n_group:
            start_group(b, g + n_slot, slot)
        else:
            @pl.when(b + 1 < n_batch)
            def _():
                start_group(b + 1, g + n_slot - n_group, slot)

    n_new = kn_ref.shape[1]
    knew[0:n_new, :] = kn_ref[r]
    vnew[0:n_new, :] = vn_ref[r]
    s_new = qk(knew[...].astype(BF16)) + tab_ref[2]
    _, l, acc = softmax_update(state, [s_new], [vnew[...].astype(BF16)])
    n = acc / l
    o_ref[r] = n - lam_ref[0] * pltpu.roll(n, rows - t_new, 0)


def _inproj_decode_kernel(*refs, rows_per_step, **cfg):
    pt_ref, lam_ref = refs[:2]
    k0 = 2 + _N_INPROJ_IN
    dec_in = refs[k0:k0 + 6]
    k1 = k0 + 6
    o_ref = refs[k1 + _N_INPROJ_OUT]
    scratch = refs[k1 + _N_INPROJ_OUT + 1:]
    _inproj_kernel(*refs[2:k0], *refs[k1:k1 + _N_INPROJ_OUT])
    for r in range(rows_per_step):
        b = pl.program_id(0) * rows_per_step + r

        @pl.when(b < cfg["n_batch"])
        def _():
            _decode_row(b, r, pt_ref, lam_ref, *dec_in, o_ref, *scratch, **cfg)


def _in_projection_with_decode(x2d, g_pre, w_bf16, cos_t, sin_t, tm,
                               page_table, lam, q_rows, k_new, v_new, tabs, cache_k, cache_v,
                               group=8, n_slot=4):
    n_steps = x2d.shape[0] // tm
    n_batch, rows, _ = q_rows.shape
    n_pages = page_table.shape[1]
    width = cache_k.shape[1]
    group = math.gcd(group, n_pages)
    n_slot = max(1, math.gcd(n_slot, n_pages // group))
    rows_per_step = -(-n_batch // n_steps)
    n_blocks = n_batch // rows_per_step
    assert n_blocks * rows_per_step == n_batch, "sample rows must split evenly over the grid steps"
    per_b = lambda a: pl.BlockSpec((rows_per_step,) + a.shape[1:],
                                   lambda i: (jnp.minimum(i, n_blocks - 1), 0, 0))
    in_specs, out_specs, out_shape = _inproj_specs(x2d, w_bf16, cos_t, tm)
    smem = pl.BlockSpec(memory_space=pltpu.SMEM)
    res = pl.pallas_call(
        functools.partial(_inproj_decode_kernel, rows_per_step=rows_per_step, n_batch=n_batch,
                          n_pages=n_pages, group=group, n_slot=n_slot, t_new=rows // (2 * H_D)),
        grid=(n_steps,),
        in_specs=[smem, smem] + in_specs + [per_b(q_rows), per_b(k_new), per_b(v_new),
                                             pl.BlockSpec(tabs.shape, lambda i: (0, 0, 0)),
                                             pl.BlockSpec(memory_space=pl.ANY),
                                             pl.BlockSpec(memory_space=pl.ANY)],
        out_specs=out_specs + [per_b(q_rows)],
        out_shape=out_shape + [jax.ShapeDtypeStruct(q_rows.shape, F32)],
        scratch_shapes=[pltpu.VMEM((n_slot, group, width, LANES), F32),
                        pltpu.VMEM((n_slot, group, width, LANES), F32),
                        pltpu.VMEM((width, LANES), F32),
                        pltpu.VMEM((width, LANES), F32),
                        pltpu.SemaphoreType.DMA((2, n_slot))],
        compiler_params=_cparams(1),
        name="in_projection_with_decode",
    )(page_table.reshape(-1), lam, x2d, g_pre, w_bf16, cos_t, sin_t, q_rows, k_new, v_new, tabs,
      cache_k, cache_v)
    return res[:_N_INPROJ_OUT], res[_N_INPROJ_OUT]


def _sample_ret_kernel(qt_ref, kt_ref, v_ref, s_ref, o_ref, s_out_ref, *, gamma, bb, t_new):
    for i in range(bb):
        v = v_ref[i]
        for h in range(H_R):
            st = s_ref[i, h]
            qt = qt_ref[i, h]
            kt = kt_ref[i, h]
            outs = []
            for t in range(t_new):
                vrow = v[t:t + 1, h * DV_R:(h + 1) * DV_R]
                st = gamma[h] * st + kt[:, t:t + 1] * vrow
                outs.append(jnp.sum(qt[:, t:t + 1] * st, axis=0, keepdims=True))
            o_ref[i, :, h * DV_R:(h + 1) * DV_R] = jnp.concatenate(outs, axis=0)
            s_out_ref[i, h] = st


def _sample_retention(qr_t, kr_t, vr, state, bb=8):
    n_batch, _, _, t_new = qr_t.shape
    bb = math.gcd(bb, n_batch)
    gamma = tuple(float(v) for v in np.exp(_log_gamma_np()))
    m4 = lambda i: (i, 0, 0, 0)
    m3 = lambda i: (i, 0, 0)
    return pl.pallas_call(
        functools.partial(_sample_ret_kernel, gamma=gamma, bb=bb, t_new=t_new),
        grid=(n_batch // bb,),
        in_specs=[pl.BlockSpec((bb, H_R, DK_R, t_new), m4),
                  pl.BlockSpec((bb, H_R, DK_R, t_new), m4),
                  pl.BlockSpec((bb, t_new, W_R), m3),
                  pl.BlockSpec((bb, H_R, DK_R, DV_R), m4)],
        out_specs=[pl.BlockSpec((bb, t_new, W_R), m3),
                   pl.BlockSpec((bb, H_R, DK_R, DV_R), m4)],
        out_shape=[jax.ShapeDtypeStruct((n_batch, t_new, W_R), F32),
                   jax.ShapeDtypeStruct(state.shape, F32)],
        compiler_params=_cparams(1),
        name="sample_retention",
    )(qr_t, kr_t, vr, state)


def _merge_kernel(x_ref, od_ref, or_ref, gd_ref, gr_ref, gnd_ref, gnr_ref, w_ref, gpost_ref, y_ref):
    def gated(o_ref_, g_ref_, gn):
        parts = []
        for h in range(o_ref_.shape[1] // LANES):
            cols = slice(h * LANES, (h + 1) * LANES)
            o = o_ref_[:, cols].astype(F32)
            g = g_ref_[:, cols].astype(F32)
            y = o * lax.rsqrt(jnp.mean(o * o, axis=-1, keepdims=True) + EPS) * gn
            parts.append((y * (g * jax.nn.sigmoid(g))).astype(BF16))
        return parts

    o = jnp.concatenate(gated(od_ref, gd_ref, gnd_ref[...]) + gated(or_ref, gr_ref, gnr_ref[...]), axis=1)
    mix = jnp.dot(o, w_ref[...], preferred_element_type=F32)
    y = mix * lax.rsqrt(jnp.mean(mix * mix, axis=-1, keepdims=True) + EPS) * gpost_ref[...]
    y_ref[...] = x_ref[...] + y


def _merge(x2d, od, orr, gd, gr, gn_d, gn_r, w_bf16, g_post, tm):
    n, d = x2d.shape
    row = lambda i: (i, 0)
    fix = lambda i: (0, 0)
    return pl.pallas_call(
        _merge_kernel,
        grid=(n // tm,),
        in_specs=[pl.BlockSpec((tm, d), row),
                  pl.BlockSpec((tm, W_D), row),
                  pl.BlockSpec((tm, W_R), row),
                  pl.BlockSpec((tm, W_D), row),
                  pl.BlockSpec((tm, W_R), row),
                  pl.BlockSpec((1, DV_D), fix),
                  pl.BlockSpec((1, DV_R), fix),
                  pl.BlockSpec(w_bf16.shape, fix),
                  pl.BlockSpec((1, d), fix)],
        out_specs=pl.BlockSpec((tm, d), row),
        out_shape=jax.ShapeDtypeStruct((n, d), F32),
        compiler_params=_cparams(1),
        name="merge_out_projection",
    )(x2d, od, orr, gd, gr, gn_d, gn_r, w_bf16, g_post)


def _pick_tile(n, target):
    t = math.gcd(n, target)
    assert t % 8 == 0, (n, target)
    return t


def kernel(x_prompt, x_sample, cache_k, cache_v, state_ret, page_table, rel_bias, norm_pre, norm_post,
           w_in, w_out, lambda_q1, lambda_k1, lambda_q2, lambda_k2, gn_diff, gn_ret):
    depth = w_in.shape[0]
    batch, seq, d_model = x_prompt.shape
    dec_batch, t_new, _ = x_sample.shape
    n_pool, page = cache_k.shape[1], cache_k.shape[2]
    n_pages = page_table.shape[1]
    past = n_pages * page
    far = _far_distance()

    t_attn = _pick_tile(seq, 512)
    assert t_attn > far and page > far, "bias must be constant beyond the neighbouring tile / page"
    tm_p = _pick_tile(batch * seq, 512)
    assert seq % tm_p == 0
    tm_s = _pick_tile(dec_batch * t_new, 512)

    vals = jnp.concatenate([(rel_bias - rel_bias[NUM_BUCKETS - 1:]) * LOG2E,
                            jnp.full((1, H_D), NEG, F32)], axis=0).astype(F32)
    tabs_p = _bias_tables(vals, jnp.asarray(_prompt_bucket_codes(t_attn)))
    tabs_s = _bias_tables(vals, jnp.asarray(_decode_bucket_codes(page, t_new)))

    cos_p, sin_p = _rotation_tables(jnp.arange(seq, dtype=jnp.int32))
    pos_s = past + (jnp.arange(tm_s, dtype=jnp.int32) % t_new)
    cos_s, sin_s = _rotation_tables(pos_s)

    yp = x_prompt.reshape(batch * seq, d_model)
    ys = x_sample.reshape(dec_batch * t_new, d_model)
    kp_l, vp_l, sp_l, ks_l, vs_l, ss_l = [], [], [], [], [], []
    for layer in range(depth):
        lambda_init = 0.8 - 0.6 * math.exp(-0.3 * layer)
        lam = (jnp.exp(jnp.sum(lambda_q1[layer].astype(F32) * lambda_k1[layer].astype(F32)))
               - jnp.exp(jnp.sum(lambda_q2[layer].astype(F32) * lambda_k2[layer].astype(F32)))
               + lambda_init).reshape(1).astype(F32)
        w_in_b = w_in[layer].astype(BF16)
        w_out_b = w_out[layer].astype(BF16)
        g_pre = norm_pre[layer].reshape(1, d_model)
        g_post = norm_post[layer].reshape(1, d_model)
        gn_d = (gn_diff[layer].astype(F32) * (1.0 - lambda_init)).reshape(1, DV_D)
        gn_r = gn_ret[layer].astype(F32).reshape(1, DV_R)

        k_s, v_s, _, _, q_s, gd_s, qr_s, kr_s, vr_s, gr_s = _in_projection(
            ys, g_pre, w_in_b, cos_s, sin_s, tm_s)
        q4 = q_s.astype(F32).reshape(dec_batch, t_new, H_D, 2 * D_QK).transpose(0, 2, 1, 3)
        q_rows = jnp.concatenate([q4, q4], axis=2).reshape(dec_batch, H_D * 2 * t_new, 2 * D_QK)

        (k_p, v_p, kb_p, vb_p, q_p, gd_p, qr_p, kr_p, vr_p, gr_p), od_s = _in_projection_with_decode(
            yp, g_pre, w_in_b, cos_p, sin_p, tm_p,
            page_table + layer * n_pool, lam, q_rows,
            k_s.reshape(dec_batch, t_new * H_D, 2 * D_QK),
            v_s.reshape(dec_batch, t_new * H_D, DV_D),
            tabs_s.reshape(3, H_D * 2 * t_new, page * H_D),
            cache_k.reshape(depth * n_pool, page * H_D, 2 * D_QK),
            cache_v.reshape(depth * n_pool, page * H_D, DV_D))
        od_p = _prompt_attention(lam, q_p, kb_p, vb_p, tabs_p, batch, seq, t_attn)
        or_p, s_p = _prompt_retention(qr_p, kr_p, vr_p, batch, seq)
        yp = _merge(yp, od_p, or_p, gd_p, gr_p, gn_d, gn_r, w_out_b, g_post, tm_p)
        kp_l.append(k_p.reshape(batch, seq, H_D, 2 * D_QK))
        vp_l.append(v_p.reshape(batch, seq, H_D, DV_D))
        sp_l.append(s_p)

        od_s = od_s.reshape(dec_batch, H_D, 2, t_new, DV_D)[:, :, 0].transpose(0, 2, 1, 3)
        od_s = od_s.reshape(dec_batch * t_new, W_D)
        to_cols = lambda a: a.astype(F32).reshape(dec_batch, t_new, H_R, DK_R).transpose(0, 2, 3, 1)
        or_s, s_s = _sample_retention(to_cols(qr_s), to_cols(kr_s),
                                      vr_s.astype(F32).reshape(dec_batch, t_new, W_R),
                                      state_ret[layer].astype(F32))
        ys = _merge(ys, od_s, or_s.reshape(dec_batch * t_new, W_R), gd_s, gr_s, gn_d, gn_r, w_out_b, g_post, tm_s)
        ks_l.append(k_s.reshape(dec_batch, t_new, H_D, 2 * D_QK))
        vs_l.append(v_s.reshape(dec_batch, t_new, H_D, DV_D))
        ss_l.append(s_s)

    stack = lambda parts: parts[0][None] if len(parts) == 1 else jnp.stack(parts)
    return (yp.reshape(batch, seq, d_model), ys.reshape(dec_batch, t_new, d_model),
            stack(kp_l), stack(vp_l), stack(sp_l), stack(ks_l), stack(vs_l), stack(ss_l))
```

```python
import functools
import math

import jax
import jax.numpy as jnp
import numpy as np
from jax import lax
from jax.experimental import pallas as pl
from jax.experimental.pallas import tpu as pltpu

F32 = jnp.float32
BF16 = jnp.bfloat16

H_D = 4
D_QK = 64
DV_D = 128
W_D = H_D * DV_D
H_R = 4
DK_R = 64
DV_R = 128
W_R = H_R * DV_R
NUM_BUCKETS = 32
MAX_DISTANCE = 128
RET_CHUNK = 128
EPS = 1e-6
NEG = -1e30
LOG2E = 1.4426950408889634
Q_PRESCALE = D_QK ** -0.5 * LOG2E
MASK_CODE = NUM_BUCKETS
LANES = 128
VMEM_LIMIT = 56 * 1024 * 1024

_SIZES = [H_D * 2 * D_QK, H_D * 2 * D_QK, W_D, W_D, H_R * DK_R, H_R * DK_R, W_R, W_R]
_OFF = [0] + [int(v) for v in np.cumsum(_SIZES)]


def _cparams(n_axes):
    return pltpu.CompilerParams(dimension_semantics=("arbitrary",) * n_axes,
                                vmem_limit_bytes=VMEM_LIMIT)


def _bucket_np(rel):
    n = np.maximum(rel, 0)
    max_exact = NUM_BUCKETS // 2
    nf = np.maximum(n, max_exact).astype(np.float32)
    large = max_exact + (np.log(nf / np.float32(max_exact)) / np.float32(math.log(MAX_DISTANCE / max_exact))
                         * np.float32(NUM_BUCKETS - max_exact)).astype(np.int32)
    large = np.minimum(large, NUM_BUCKETS - 1)
    return np.where(n < max_exact, n, large).astype(np.int32)


def _far_distance():
    b = _bucket_np(np.arange(4 * MAX_DISTANCE))
    return int(np.max(np.nonzero(b != NUM_BUCKETS - 1)[0])) + 1


def _prompt_bucket_codes(t):
    r = np.arange(t)[:, None]
    c = np.arange(t)[None, :]
    far = np.full((t, t), NUM_BUCKETS - 1)
    left = _bucket_np(r - c + t)
    diag = np.where(r >= c, _bucket_np(r - c), MASK_CODE)
    return np.stack([far, left, diag]).astype(np.int32)[:, None]


def _decode_bucket_codes(page, t_new):
    h = np.arange(H_D)[:, None, None]
    t = np.tile(np.arange(t_new), 2)[None, :, None]
    col = np.arange(page * H_D)[None, None, :]
    j, hk = col // H_D, col % H_D
    own = hk == h
    far = np.where(own, NUM_BUCKETS - 1, MASK_CODE) + 0 * t
    last = np.where(own, _bucket_np(page + t - j), MASK_CODE)
    new = np.where(own & (j <= t) & (j < t_new), _bucket_np(t - j), MASK_CODE)
    return np.stack([far, last, new]).astype(np.int32)


def _rotation_tables(pos):
    half = DK_R // 2
    inv = 1.0 / (10000.0 ** jnp.linspace(0.0, 1.0, half, dtype=F32))
    ang = pos.astype(F32)[:, None] * inv[None, :]
    cos = jnp.cos(ang)
    sin = jnp.sin(ang)
    cos_t = jnp.tile(cos, (1, LANES // half))
    sin_t = jnp.tile(jnp.concatenate([-sin, sin], axis=1), (1, LANES // DK_R))
    return cos_t, sin_t


def _log_gamma_np():
    return np.log(1.0 - 2.0 ** (-5.0 - np.arange(H_R, dtype=np.float64)))


def _table_kernel(vals_ref, code_ref, out_ref):
    h = pl.program_id(1)
    code = code_ref[0, 0]
    acc = jnp.full(code.shape, vals_ref[MASK_CODE, h], F32)
    for b in range(NUM_BUCKETS):
        acc = jnp.where(code == b, vals_ref[b, h], acc)
    out_ref[0, 0] = acc


def _bias_tables(vals, codes):
    n, hc, r, c = codes.shape
    code_map = (lambda t, h: (t, h, 0, 0)) if hc == H_D else (lambda t, h: (t, 0, 0, 0))
    return pl.pallas_call(
        _table_kernel,
        grid=(n, H_D),
        in_specs=[pl.BlockSpec(memory_space=pltpu.SMEM),
                  pl.BlockSpec((1, 1, r, c), code_map)],
        out_specs=pl.BlockSpec((1, 1, r, c), lambda t, h: (t, h, 0, 0)),
        out_shape=jax.ShapeDtypeStruct((n, H_D, r, c), F32),
        compiler_params=_cparams(2),
        name="bias_tables",
    )(vals, codes)


_N_INPROJ_IN = 5
_N_INPROJ_OUT = 10


def _inproj_kernel(x_ref, g_ref, w_ref, cos_ref, sin_ref,
                   k_ref, v_ref, kb_ref, vb_ref, q_ref, gd_ref, qr_ref, kr_ref, vr_ref, gr_ref):
    tm = x_ref.shape[0]
    x = x_ref[...]
    inv = lax.rsqrt(jnp.mean(x * x, axis=-1, keepdims=True) + EPS)
    h = (x * inv * g_ref[...]).astype(BF16)

    def proj(i):
        return jnp.dot(h, w_ref[:, _OFF[i]:_OFF[i + 1]], preferred_element_type=F32)

    def store_heads_as_rows(ref, z):
        for hd in range(H_D):
            ref[pl.ds(hd, tm, stride=H_D), :] = z[:, hd * LANES:(hd + 1) * LANES]

    q_ref[...] = (proj(0) * Q_PRESCALE).astype(BF16)
    zk = proj(1)
    store_heads_as_rows(k_ref, zk)
    kb_ref[...] = zk.astype(BF16)
    zv = proj(2)
    store_heads_as_rows(v_ref, zv)
    vb_ref[...] = zv.astype(BF16)
    gd_ref[...] = proj(3).astype(BF16)

    cos = cos_ref[...]
    sin = sin_ref[...]
    lane = lax.broadcasted_iota(jnp.int32, cos.shape, 1)
    low_half = (lane % DK_R) < (DK_R // 2)

    def rotate(z):
        outs = []
        for c in range(z.shape[1] // LANES):
            zc = z[:, c * LANES:(c + 1) * LANES]
            partner = jnp.where(low_half,
                                pltpu.roll(zc, LANES - DK_R // 2, 1),
                                pltpu.roll(zc, DK_R // 2, 1))
            outs.append(zc * cos + partner * sin)
        return jnp.concatenate(outs, axis=1)

    qr_ref[...] = rotate(proj(4)).astype(BF16)
    kr_ref[...] = (rotate(proj(5)) * (DK_R ** -0.5)).astype(BF16)
    vr_ref[...] = proj(6).astype(BF16)
    gr_ref[...] = proj(7).astype(BF16)


def _inproj_specs(x2d, w_bf16, cos_t, tm):
    n, d = x2d.shape
    n_tab = cos_t.shape[0] // tm
    row = lambda i: (i, 0)
    tab = lambda i: (i % n_tab, 0)
    outs = [(H_D, 2 * D_QK, F32), (H_D, DV_D, F32), (1, W_D, BF16), (1, W_D, BF16),
            (1, 2 * H_D * D_QK, BF16), (1, W_D, BF16),
            (1, H_R * DK_R, BF16), (1, H_R * DK_R, BF16), (1, W_R, BF16), (1, W_R, BF16)]
    in_specs = [pl.BlockSpec((tm, d), row),
                pl.BlockSpec((1, d), lambda i: (0, 0)),
                pl.BlockSpec(w_bf16.shape, lambda i: (0, 0), pipeline_mode=pl.Buffered(1)),
                pl.BlockSpec((tm, LANES), tab),
                pl.BlockSpec((tm, LANES), tab)]
    out_specs = [pl.BlockSpec((tm * r, w), row) for r, w, _ in outs]
    out_shape = [jax.ShapeDtypeStruct((n * r, w), dt) for r, w, dt in outs]
    return in_specs, out_specs, out_shape


def _in_projection(x2d, g_pre, w_bf16, cos_t, sin_t, tm):
    in_specs, out_specs, out_shape = _inproj_specs(x2d, w_bf16, cos_t, tm)
    return pl.pallas_call(
        _inproj_kernel,
        grid=(x2d.shape[0] // tm,),
        in_specs=in_specs,
        out_specs=out_specs,
        out_shape=out_shape,
        compiler_params=_cparams(1),
        name="in_projection",
    )(x2d, g_pre, w_bf16, cos_t, sin_t)


def _attn_kernel(lam_ref, qi_ref, kj_ref, q_ref, kb, v_ref, tab_ref, o_ref, vb, s_bufs, mx_bufs, m_all, acc_all,
                 *, t, n_pairs, unroll):
    n_buf = s_bufs.shape[0]
    assert unroll % n_buf == 0
    s_len = kb.shape[0]
    nq = s_len // t

    @pl.when((pl.program_id(0) == 0) & (pl.program_id(1) == 0))
    def _():
        m_all[...] = jnp.full(m_all.shape, NEG, F32)
        acc_all[...] = jnp.zeros(acc_all.shape, F32)

    vb[:, :DV_D] = v_ref[...]
    vb[:, DV_D:] = jnp.ones((s_len, DV_D), BF16)
    lane = lax.broadcasted_iota(jnp.int32, (t, 2 * D_QK), 1)

    def scores_into(slot, n):
        q = q_ref[pl.ds(pl.multiple_of(qi_ref[n] * t, t), t), :]
        zero = jnp.zeros_like(q)
        qs = jnp.concatenate([jnp.where(lane < D_QK, q, zero), jnp.where(lane >= D_QK, q, zero)], axis=0)
        keys = kb[pl.ds(pl.multiple_of(kj_ref[n] * t, t), t), :]
        tab = tab_ref[jnp.clip(kj_ref[n] - qi_ref[n] + 2, 0, 2), 0]
        s = (lax.dot_general(qs, keys, (((1,), (1,)), ((), ())), preferred_element_type=F32)
             + jnp.concatenate([tab, tab], axis=0))
        s_bufs[slot] = s
        mx = s[:, :LANES]
        for c in range(1, t // LANES):
            mx = jnp.maximum(mx, s[:, c * LANES:(c + 1) * LANES])
        mx_bufs[slot] = mx

    def consume(slot, n):
        i = qi_ref[n]
        j = kj_ref[n]
        m_prev = m_all[i]
        m_new = jnp.maximum(m_prev, jnp.max(mx_bufs[slot], axis=1, keepdims=True))
        corr = jnp.exp2(m_prev - m_new)
        p = jnp.exp2(s_bufs[slot] - jnp.concatenate([m_new] * (t // LANES), axis=1))
        pv = jnp.dot(p.astype(BF16), vb[pl.ds(pl.multiple_of(j * t, t), t), :], preferred_element_type=F32)
        acc_all[i] = acc_all[i] * jnp.concatenate([corr, corr], axis=1) + pv
        m_all[i] = m_new

    scores_into(0, 0)

    def sweep(h, carry):
        n = unroll * h
        for u in range(unroll):
            scores_into((u + 1) % n_buf, n + u + 1)
            consume(u % n_buf, n + u)
        return carry

    lax.fori_loop(0, n_pairs // unroll, sweep, 0)
    for n in range(n_pairs - n_pairs % unroll, n_pairs):
        scores_into((n + 1) % n_buf, n + 1)
        consume(n % n_buf, n)

    for i in range(nq):
        acc = acc_all[i]
        nrm = acc[:, :DV_D] / acc[:, DV_D:]
        o_ref[i * t:(i + 1) * t, :] = (nrm[:t] - lam_ref[0] * nrm[t:]).astype(o_ref.dtype)
        m_all[i] = jnp.full(m_all.shape[1:], NEG, F32)


def _prompt_attention(lam, q2d, k2d, v2d, tabs, batch, seq, t, n_buf=3, unroll=6):
    nq = seq // t
    pairs = [(i, j) for i in range(nq) for j in range(i + 1)]
    pairs.append(pairs[-1])
    qi = jnp.asarray([p[0] for p in pairs], jnp.int32)
    kj = jnp.asarray([p[1] for p in pairs], jnp.int32)
    bh = lambda b, h: (b, h)
    smem = pl.BlockSpec(memory_space=pltpu.SMEM)
    return pl.pallas_call(
        functools.partial(_attn_kernel, t=t, n_pairs=len(pairs) - 1, unroll=unroll),
        grid=(batch, H_D),
        in_specs=[smem, smem, smem,
                  pl.BlockSpec((seq, 2 * D_QK), bh),
                  pl.BlockSpec((seq, 2 * D_QK), bh),
                  pl.BlockSpec((seq, DV_D), bh),
                  pl.BlockSpec((3, 1, t, t), lambda b, h: (0, h, 0, 0))],
        out_specs=pl.BlockSpec((seq, DV_D), bh),
        out_shape=jax.ShapeDtypeStruct((batch * seq, W_D), BF16),
        scratch_shapes=[pltpu.VMEM((seq, 2 * DV_D), BF16),
                        pltpu.VMEM((n_buf, 2 * t, t), F32),
                        pltpu.VMEM((n_buf, 2 * t, LANES), F32),
                        pltpu.VMEM((nq, 2 * t, LANES), F32),
                        pltpu.VMEM((nq, 2 * t, 2 * DV_D), F32)],
        compiler_params=_cparams(2),
        name="prompt_attention",
    )(lam, qi, kj, q2d, k2d, v2d, tabs)


def _ret_kernel(qr_ref, kr_ref, vr_ref, dec_ref, qdec_ref, kdec_ref, o_ref, s_out_ref, st_sc, *, g_chunk, bb):
    c = pl.program_id(1)
    chunk = qr_ref.shape[1]

    @pl.when(c == 0)
    def _():
        st_sc[...] = jnp.zeros(st_sc.shape, F32)

    head_of_lane = lax.broadcasted_iota(jnp.int32, (chunk, H_R * DK_R), 1) // DK_R
    for i in range(bb):
        q = qr_ref[i]
        k = kr_ref[i]
        v = vr_ref[i]
        st = st_sc[i]
        q_all = jnp.concatenate([jnp.where(head_of_lane == h, q, jnp.zeros_like(q)) for h in range(H_R)], axis=0)
        sc_all = lax.dot_general(q_all, k, (((1,), (1,)), ((), ())), preferred_element_type=F32) * dec_ref[...]
        cross_all = jnp.dot(q_all, st.astype(BF16), preferred_element_type=F32) * qdec_ref[...]
        kd_t = (k.astype(F32) * kdec_ref[...]).T
        for h in range(H_R):
            tok = slice(h * chunk, (h + 1) * chunk)
            rows = slice(h * DK_R, (h + 1) * DK_R)
            vh = v[:, h * DV_R:(h + 1) * DV_R]
            inner = jnp.dot(sc_all[tok].astype(BF16), vh, preferred_element_type=F32)
            o_ref[i, :, h * DV_R:(h + 1) * DV_R] = (inner + cross_all[tok]).astype(o_ref.dtype)
            upd = jnp.dot(kd_t[rows, :].astype(BF16), vh, preferred_element_type=F32)
            st_new = g_chunk[h] * st[rows, :] + upd
            st_sc[i, rows, :] = st_new
            s_out_ref[i, h] = st_new


def _prompt_retention(qr, kr, vr, batch, seq, bb=4):
    c = min(RET_CHUNK, seq)
    nc = seq // c
    bb = math.gcd(bb, batch)
    lg = _log_gamma_np()
    idx = np.arange(c, dtype=np.float64)
    d = idx[:, None] - idx[None, :]
    decay = np.where(d[None] >= 0, np.exp(lg[:, None, None] * np.maximum(d, 0.0)[None]), 0.0)
    qdec = np.broadcast_to(np.exp(lg[:, None] * (idx + 1.0)[None, :])[:, :, None], (H_R, c, DV_R))
    kdec = np.repeat(np.exp(lg[:, None] * (c - 1.0 - idx)[None, :]).T, DK_R, axis=1)
    g_chunk = tuple(float(v) for v in np.exp(lg * c))
    blk = lambda w: pl.BlockSpec((bb, c, w), lambda b, j: (b, j, 0))
    fix = lambda b, j: (0, 0)
    as3 = lambda a: a.reshape(batch, seq, a.shape[-1])
    o, s = pl.pallas_call(
        functools.partial(_ret_kernel, g_chunk=g_chunk, bb=bb),
        grid=(batch // bb, nc),
        in_specs=[blk(H_R * DK_R), blk(H_R * DK_R), blk(W_R),
                  pl.BlockSpec((H_R * c, c), fix),
                  pl.BlockSpec((H_R * c, DV_R), fix),
                  pl.BlockSpec((c, H_R * DK_R), fix)],
        out_specs=[blk(W_R),
                   pl.BlockSpec((bb, H_R, DK_R, DV_R), lambda b, j: (b, 0, 0, 0))],
        out_shape=[jax.ShapeDtypeStruct((batch, seq, W_R), BF16),
                   jax.ShapeDtypeStruct((batch, H_R, DK_R, DV_R), F32)],
        scratch_shapes=[pltpu.VMEM((bb, H_R * DK_R, DV_R), F32)],
        compiler_params=_cparams(2),
        name="prompt_retention",
    )(as3(qr), as3(kr), as3(vr), jnp.asarray(decay.reshape(H_R * c, c), F32),
      jnp.asarray(qdec.reshape(H_R * c, DV_R), F32), jnp.asarray(kdec, F32))
    return o.reshape(batch * seq, W_R), s


def _decode_row(b, r, pt_ref, lam_ref, q_ref, kn_ref, vn_ref, tab_ref, ck_hbm, cv_hbm, o_ref,
                kbuf, vbuf, knew, vnew, sem, *, n_batch, n_pages, group, n_slot, t_new):
    n_group = n_pages // group
    rows = H_D * 2 * t_new
    width = kbuf.shape[2]

    def page_copies(bb, g, slot, real):
        out = []
        for p in range(group):
            phys = pt_ref[bb * n_pages + g * group + p] if real else 0
            out.append(pltpu.make_async_copy(ck_hbm.at[phys], kbuf.at[slot, p], sem.at[0, slot]))
            out.append(pltpu.make_async_copy(cv_hbm.at[phys], vbuf.at[slot, p], sem.at[1, slot]))
        return out

    def start_group(bb, g, slot):
        for cp in page_copies(bb, g, slot, True):
            cp.start()

    @pl.when(b == 0)
    def _():
        for g0 in range(n_slot):
            start_group(0, g0, g0)
        knew[...] = jnp.zeros(knew.shape, F32)
        vnew[...] = jnp.zeros(vnew.shape, F32)

    row = lax.broadcasted_iota(jnp.int32, (rows, 2 * D_QK), 0)
    lane = lax.broadcasted_iota(jnp.int32, (rows, 2 * D_QK), 1)
    own_half = ((row // t_new) % 2 == 0) == (lane < D_QK)
    q = jnp.where(own_half, q_ref[r], 0.0).astype(BF16)

    def softmax_update(state, scores, values):
        m, l, acc = state
        mx = scores[0]
        for s in scores[1:]:
            mx = jnp.maximum(mx, s)
        m_new = jnp.maximum(m, jnp.broadcast_to(jnp.max(mx, axis=1, keepdims=True), m.shape))
        corr = jnp.exp2(m - m_new)
        m_wide = jnp.concatenate([m_new] * (width // LANES), axis=1)
        ps = [jnp.exp2(s - m_wide) for s in scores]
        psum = ps[0]
        for p in ps[1:]:
            psum = psum + p
        l_new = l * corr + jnp.broadcast_to(jnp.sum(psum, axis=1, keepdims=True), l.shape)
        pv = jnp.dot(ps[0].astype(BF16), values[0], preferred_element_type=F32)
        for p, vv in zip(ps[1:], values[1:]):
            pv = pv + jnp.dot(p.astype(BF16), vv, preferred_element_type=F32)
        return m_new, l_new, acc * corr + pv

    def qk(keys):
        return lax.dot_general(q, keys, (((1,), (1,)), ((), ())), preferred_element_type=F32)

    state = (jnp.full((rows, LANES), NEG, F32), jnp.zeros((rows, LANES), F32), jnp.zeros((rows, DV_D), F32))
    for g in range(n_group):
        slot = g % n_slot if n_group % n_slot == 0 else lax.rem(b * n_group + g, n_slot)
        for cp in page_copies(b, g, slot, False):
            cp.wait()
        tabs = [tab_ref[0]] * (group - 1) + [tab_ref[1 if g == n_group - 1 else 0]]
        scores = [qk(kbuf[slot, p].astype(BF16)) + tabs[p] for p in range(group)]
        values = [vbuf[slot, p].astype(BF16) for p in range(group)]
        state = softmax_update(state, scores, values)
        if g + n_slot < n_group:
            start_group(b, g + n_slot, slot)
        else:
            @pl.when(b + 1 < n_batch)
            def _():
                start_group(b + 1, g + n_slot - n_group, slot)

    n_new = kn_ref.shape[1]
    knew[0:n_new, :] = kn_ref[r]
    vnew[0:n_new, :] = vn_ref[r]
    s_new = qk(knew[...].astype(BF16)) + tab_ref[2]
    _, l, acc = softmax_update(state, [s_new], [vnew[...].astype(BF16)])
    n = acc / l
    o_ref[r] = n - lam_ref[0] * pltpu.roll(n, rows - t_new, 0)


def _inproj_decode_kernel(*refs, rows_per_step, **cfg):
    pt_ref, lam_ref = refs[:2]
    k0 = 2 + _N_INPROJ_IN
    dec_in = refs[k0:k0 + 6]
    k1 = k0 + 6
    o_ref = refs[k1 + _N_INPROJ_OUT]
    scratch = refs[k1 + _N_INPROJ_OUT + 1:]
    _inproj_kernel(*refs[2:k0], *refs[k1:k1 + _N_INPROJ_OUT])
    for r in range(rows_per_step):
        b = pl.program_id(0) * rows_per_step + r

        @pl.when(b < cfg["n_batch"])
        def _():
            _decode_row(b, r, pt_ref, lam_ref, *dec_in, o_ref, *scratch, **cfg)


def _in_projection_with_decode(x2d, g_pre, w_bf16, cos_t, sin_t, tm,
                               page_table, lam, q_rows, k_new, v_new, tabs, cache_k, cache_v,
                               group=8, n_slot=6):
    n_steps = x2d.shape[0] // tm
    n_batch, rows, _ = q_rows.shape
    n_pages = page_table.shape[1]
    width = cache_k.shape[1]
    group = math.gcd(group, n_pages)
    n_slot = min(n_slot, n_pages // group)
    rows_per_step = -(-n_batch // n_steps)
    n_blocks = n_batch // rows_per_step
    assert n_blocks * rows_per_step == n_batch, "sample rows must split evenly over the grid steps"
    per_b = lambda a: pl.BlockSpec((rows_per_step,) + a.shape[1:],
                                   lambda i: (jnp.minimum(i, n_blocks - 1), 0, 0))
    in_specs, out_specs, out_shape = _inproj_specs(x2d, w_bf16, cos_t, tm)
    smem = pl.BlockSpec(memory_space=pltpu.SMEM)
    res = pl.pallas_call(
        functools.partial(_inproj_decode_kernel, rows_per_step=rows_per_step, n_batch=n_batch,
                          n_pages=n_pages, group=group, n_slot=n_slot, t_new=rows // (2 * H_D)),
        grid=(n_steps,),
        in_specs=[smem, smem] + in_specs + [per_b(q_rows), per_b(k_new), per_b(v_new),
                                             pl.BlockSpec(tabs.shape, lambda i: (0, 0, 0)),
                                             pl.BlockSpec(memory_space=pl.ANY),
                                             pl.BlockSpec(memory_space=pl.ANY)],
        out_specs=out_specs + [per_b(q_rows)],
        out_shape=out_shape + [jax.ShapeDtypeStruct(q_rows.shape, F32)],
        scratch_shapes=[pltpu.VMEM((n_slot, group, width, LANES), F32),
                        pltpu.VMEM((n_slot, group, width, LANES), F32),
                        pltpu.VMEM((width, LANES), F32),
                        pltpu.VMEM((width, LANES), F32),
                        pltpu.SemaphoreType.DMA((2, n_slot))],
        compiler_params=_cparams(1),
        name="in_projection_with_decode",
    )(page_table.reshape(-1), lam, x2d, g_pre, w_bf16, cos_t, sin_t, q_rows, k_new, v_new, tabs,
      cache_k, cache_v)
    return res[:_N_INPROJ_OUT], res[_N_INPROJ_OUT]


def _sample_ret_kernel(qt_ref, kt_ref, v_ref, s_ref, o_ref, s_out_ref, *, gamma, bb, t_new):
    for i in range(bb):
        v = v_ref[i]
        for h in range(H_R):
            st = s_ref[i, h]
            qt = qt_ref[i, h]
            kt = kt_ref[i, h]
            outs = []
            for t in range(t_new):
                vrow = v[t:t + 1, h * DV_R:(h + 1) * DV_R]
                st = gamma[h] * st + kt[:, t:t + 1] * vrow
                outs.append(jnp.sum(qt[:, t:t + 1] * st, axis=0, keepdims=True))
            o_ref[i, :, h * DV_R:(h + 1) * DV_R] = jnp.concatenate(outs, axis=0)
            s_out_ref[i, h] = st


def _sample_retention(qr_t, kr_t, vr, state, bb=8):
    n_batch, _, _, t_new = qr_t.shape
    bb = math.gcd(bb, n_batch)
    gamma = tuple(float(v) for v in np.exp(_log_gamma_np()))
    m4 = lambda i: (i, 0, 0, 0)
    m3 = lambda i: (i, 0, 0)
    return pl.pallas_call(
        functools.partial(_sample_ret_kernel, gamma=gamma, bb=bb, t_new=t_new),
        grid=(n_batch // bb,),
        in_specs=[pl.BlockSpec((bb, H_R, DK_R, t_new), m4),
                  pl.BlockSpec((bb, H_R, DK_R, t_new), m4),
                  pl.BlockSpec((bb, t_new, W_R), m3),
                  pl.BlockSpec((bb, H_R, DK_R, DV_R), m4)],
        out_specs=[pl.BlockSpec((bb, t_new, W_R), m3),
                   pl.BlockSpec((bb, H_R, DK_R, DV_R), m4)],
        out_shape=[jax.ShapeDtypeStruct((n_batch, t_new, W_R), F32),
                   jax.ShapeDtypeStruct(state.shape, F32)],
        compiler_params=_cparams(1),
        name="sample_retention",
    )(qr_t, kr_t, vr, state)


def _merge_kernel(x_ref, od_ref, or_ref, gd_ref, gr_ref, gnd_ref, gnr_ref, w_ref, gpost_ref, y_ref, *, chain):
    def gated(rows, o_ref_, g_ref_, gn):
        parts = []
        for h in range(o_ref_.shape[1] // LANES):
            cols = slice(h * LANES, (h + 1) * LANES)
            o = o_ref_[rows, cols].astype(F32)
            g = g_ref_[rows, cols].astype(F32)
            y = o * lax.rsqrt(jnp.mean(o * o, axis=-1, keepdims=True) + EPS) * gn
            parts.append((y * (g * jax.nn.sigmoid(g))).astype(BF16))
        return parts

    for c in range(x_ref.shape[0] // chain):
        rows = slice(c * chain, (c + 1) * chain)
        o = jnp.concatenate(gated(rows, od_ref, gd_ref, gnd_ref[...])
                            + gated(rows, or_ref, gr_ref, gnr_ref[...]), axis=1)
        mix = jnp.dot(o, w_ref[...], preferred_element_type=F32)
        y = mix * lax.rsqrt(jnp.mean(mix * mix, axis=-1, keepdims=True) + EPS) * gpost_ref[...]
        y_ref[rows, :] = x_ref[rows, :] + y


def _merge(x2d, od, orr, gd, gr, gn_d, gn_r, w_bf16, g_post, chain, n_chain=2):
    n, d = x2d.shape
    tm = chain * math.gcd(n_chain, n // chain)
    row = lambda i: (i, 0)
    fix = lambda i: (0, 0)
    return pl.pallas_call(
        functools.partial(_merge_kernel, chain=chain),
        grid=(n // tm,),
        in_specs=[pl.BlockSpec((tm, d), row),
                  pl.BlockSpec((tm, W_D), row),
                  pl.BlockSpec((tm, W_R), row),
                  pl.BlockSpec((tm, W_D), row),
                  pl.BlockSpec((tm, W_R), row),
                  pl.BlockSpec((1, DV_D), fix),
                  pl.BlockSpec((1, DV_R), fix),
                  pl.BlockSpec(w_bf16.shape, fix),
                  pl.BlockSpec((1, d), fix)],
        out_specs=pl.BlockSpec((tm, d), row),
        out_shape=jax.ShapeDtypeStruct((n, d), F32),
        compiler_params=_cparams(1),
        name="merge_out_projection",
    )(x2d, od, orr, gd, gr, gn_d, gn_r, w_bf16, g_post)


def _pick_tile(n, target):
    t = math.gcd(n, target)
    assert t % 8 == 0, (n, target)
    return t


def kernel(x_prompt, x_sample, cache_k, cache_v, state_ret, page_table, rel_bias, norm_pre, norm_post,
           w_in, w_out, lambda_q1, lambda_k1, lambda_q2, lambda_k2, gn_diff, gn_ret):
    depth = w_in.shape[0]
    batch, seq, d_model = x_prompt.shape
    dec_batch, t_new, _ = x_sample.shape
    n_pool, page = cache_k.shape[1], cache_k.shape[2]
    n_pages = page_table.shape[1]
    past = n_pages * page
    far = _far_distance()

    t_attn = _pick_tile(seq, 512)
    assert t_attn > far and page > far, "bias must be constant beyond the neighbouring tile / page"
    tm_p = _pick_tile(batch * seq, 512)
    assert seq % tm_p == 0
    tm_s = _pick_tile(dec_batch * t_new, 512)

    vals = jnp.concatenate([(rel_bias - rel_bias[NUM_BUCKETS - 1:]) * LOG2E,
                            jnp.full((1, H_D), NEG, F32)], axis=0).astype(F32)
    tabs_p = _bias_tables(vals, jnp.asarray(_prompt_bucket_codes(t_attn)))
    tabs_s = _bias_tables(vals, jnp.asarray(_decode_bucket_codes(page, t_new)))

    cos_p, sin_p = _rotation_tables(jnp.arange(seq, dtype=jnp.int32))
    pos_s = past + (jnp.arange(tm_s, dtype=jnp.int32) % t_new)
    cos_s, sin_s = _rotation_tables(pos_s)

    yp = x_prompt.reshape(batch * seq, d_model)
    ys = x_sample.reshape(dec_batch * t_new, d_model)
    kp_l, vp_l, sp_l, ks_l, vs_l, ss_l = [], [], [], [], [], []
    for layer in range(depth):
        lambda_init = 0.8 - 0.6 * math.exp(-0.3 * layer)
        lam = (jnp.exp(jnp.sum(lambda_q1[layer].astype(F32) * lambda_k1[layer].astype(F32)))
               - jnp.exp(jnp.sum(lambda_q2[layer].astype(F32) * lambda_k2[layer].astype(F32)))
               + lambda_init).reshape(1).astype(F32)
        w_in_b = w_in[layer].astype(BF16)
        w_out_b = w_out[layer].astype(BF16)
        g_pre = norm_pre[layer].reshape(1, d_model)
        g_post = norm_post[layer].reshape(1, d_model)
        gn_d = (gn_diff[layer].astype(F32) * (1.0 - lambda_init)).reshape(1, DV_D)
        gn_r = gn_ret[layer].astype(F32).reshape(1, DV_R)

        k_s, v_s, _, _, q_s, gd_s, qr_s, kr_s, vr_s, gr_s = _in_projection(
            ys, g_pre, w_in_b, cos_s, sin_s, tm_s)
        q4 = q_s.astype(F32).reshape(dec_batch, t_new, H_D, 2 * D_QK).transpose(0, 2, 1, 3)
        q_rows = jnp.concatenate([q4, q4], axis=2).reshape(dec_batch, H_D * 2 * t_new, 2 * D_QK)

        (k_p, v_p, kb_p, vb_p, q_p, gd_p, qr_p, kr_p, vr_p, gr_p), od_s = _in_projection_with_decode(
            yp, g_pre, w_in_b, cos_p, sin_p, tm_p,
            page_table + layer * n_pool, lam, q_rows,
            k_s.reshape(dec_batch, t_new * H_D, 2 * D_QK),
            v_s.reshape(dec_batch, t_new * H_D, DV_D),
            tabs_s.reshape(3, H_D * 2 * t_new, page * H_D),
            cache_k.reshape(depth * n_pool, page * H_D, 2 * D_QK),
            cache_v.reshape(depth * n_pool, page * H_D, DV_D))
        od_p = _prompt_attention(lam, q_p, kb_p, vb_p, tabs_p, batch, seq, t_attn)
        or_p, s_p = _prompt_retention(qr_p, kr_p, vr_p, batch, seq)
        yp = _merge(yp, od_p, or_p, gd_p, gr_p, gn_d, gn_r, w_out_b, g_post, tm_p)
        kp_l.append(k_p.reshape(batch, seq, H_D, 2 * D_QK))
        vp_l.append(v_p.reshape(batch, seq, H_D, DV_D))
        sp_l.append(s_p)

        od_s = od_s.reshape(dec_batch, H_D, 2, t_new, DV_D)[:, :, 0].transpose(0, 2, 1, 3)
        od_s = od_s.reshape(dec_batch * t_new, W_D)
        to_cols = lambda a: a.astype(F32).reshape(dec_batch, t_new, H_R, DK_R).transpose(0, 2, 3, 1)
        or_s, s_s = _sample_retention(to_cols(qr_s), to_cols(kr_s),
                                      vr_s.astype(F32).reshape(dec_batch, t_new, W_R),
                                      state_ret[layer].astype(F32))
        ys = _merge(ys, od_s, or_s.reshape(dec_batch * t_new, W_R), gd_s, gr_s, gn_d, gn_r, w_out_b, g_post, tm_s)
        ks_l.append(k_s.reshape(dec_batch, t_new, H_D, 2 * D_QK))
        vs_l.append(v_s.reshape(dec_batch, t_new, H_D, DV_D))
        ss_l.append(s_s)

    stack = lambda parts: parts[0][None] if len(parts) == 1 else jnp.stack(parts)
    return (yp.reshape(batch, seq, d_model), ys.reshape(dec_batch, t_new, d_model),
            stack(kp_l), stack(vp_l), stack(sp_l), stack(ks_l), stack(vs_l), stack(ss_l))
```

```python
import functools
import math

import jax
import jax.numpy as jnp
import numpy as np
from jax import lax
from jax.experimental import pallas as pl
from jax.experimental.pallas import tpu as pltpu

F32 = jnp.float32
BF16 = jnp.bfloat16

H_D = 4
D_QK = 64
DV_D = 128
W_D = H_D * DV_D
H_R = 4
DK_R = 64
DV_R = 128
W_R = H_R * DV_R
NUM_BUCKETS = 32
MAX_DISTANCE = 128
RET_CHUNK = 128
EPS = 1e-6
NEG = -1e30
LOG2E = 1.4426950408889634
Q_PRESCALE = D_QK ** -0.5 * LOG2E
MASK_CODE = NUM_BUCKETS
LANES = 128
VMEM_LIMIT = 56 * 1024 * 1024

_SIZES = [H_D * 2 * D_QK, H_D * 2 * D_QK, W_D, W_D, H_R * DK_R, H_R * DK_R, W_R, W_R]
_OFF = [0] + [int(v) for v in np.cumsum(_SIZES)]


def _cparams(n_axes):
    return pltpu.CompilerParams(dimension_semantics=("arbitrary",) * n_axes,
                                vmem_limit_bytes=VMEM_LIMIT)


def _bucket_np(rel):
    n = np.maximum(rel, 0)
    max_exact = NUM_BUCKETS // 2
    nf = np.maximum(n, max_exact).astype(np.float32)
    large = max_exact + (np.log(nf / np.float32(max_exact)) / np.float32(math.log(MAX_DISTANCE / max_exact))
                         * np.float32(NUM_BUCKETS - max_exact)).astype(np.int32)
    large = np.minimum(large, NUM_BUCKETS - 1)
    return np.where(n < max_exact, n, large).astype(np.int32)


def _far_distance():
    b = _bucket_np(np.arange(4 * MAX_DISTANCE))
    return int(np.max(np.nonzero(b != NUM_BUCKETS - 1)[0])) + 1


def _prompt_bucket_codes(t):
    r = np.arange(t)[:, None]
    c = np.arange(t)[None, :]
    far = np.full((t, t), NUM_BUCKETS - 1)
    left = _bucket_np(r - c + t)
    diag = np.where(r >= c, _bucket_np(r - c), MASK_CODE)
    return np.stack([far, left, diag]).astype(np.int32)[:, None]


def _decode_bucket_codes(page, t_new):
    h = np.arange(H_D)[:, None, None]
    t = np.tile(np.arange(t_new), 2)[None, :, None]
    col = np.arange(page * H_D)[None, None, :]
    j, hk = col // H_D, col % H_D
    own = hk == h
    far = np.where(own, NUM_BUCKETS - 1, MASK_CODE) + 0 * t
    last = np.where(own, _bucket_np(page + t - j), MASK_CODE)
    new = np.where(own & (j <= t) & (j < t_new), _bucket_np(t - j), MASK_CODE)
    return np.stack([far, last, new]).astype(np.int32)


def _rotation_tables(pos):
    half = DK_R // 2
    inv = 1.0 / (10000.0 ** jnp.linspace(0.0, 1.0, half, dtype=F32))
    ang = pos.astype(F32)[:, None] * inv[None, :]
    cos = jnp.cos(ang)
    sin = jnp.sin(ang)
    cos_t = jnp.tile(cos, (1, LANES // half))
    sin_t = jnp.tile(jnp.concatenate([-sin, sin], axis=1), (1, LANES // DK_R))
    return cos_t, sin_t


def _log_gamma_np():
    return np.log(1.0 - 2.0 ** (-5.0 - np.arange(H_R, dtype=np.float64)))


def _table_kernel(vals_ref, code_ref, out_ref):
    h = pl.program_id(1)
    code = code_ref[0, 0]
    acc = jnp.full(code.shape, vals_ref[MASK_CODE, h], F32)
    for b in range(NUM_BUCKETS):
        acc = jnp.where(code == b, vals_ref[b, h], acc)
    out_ref[0, 0] = acc


def _bias_tables(vals, codes):
    n, hc, r, c = codes.shape
    code_map = (lambda t, h: (t, h, 0, 0)) if hc == H_D else (lambda t, h: (t, 0, 0, 0))
    return pl.pallas_call(
        _table_kernel,
        grid=(n, H_D),
        in_specs=[pl.BlockSpec(memory_space=pltpu.SMEM),
                  pl.BlockSpec((1, 1, r, c), code_map)],
        out_specs=pl.BlockSpec((1, 1, r, c), lambda t, h: (t, h, 0, 0)),
        out_shape=jax.ShapeDtypeStruct((n, H_D, r, c), F32),
        compiler_params=_cparams(2),
        name="bias_tables",
    )(vals, codes)


_N_INPROJ_IN = 5
_N_INPROJ_OUT = 10


def _inproj_kernel(x_ref, g_ref, w_ref, cos_ref, sin_ref,
                   k_ref, v_ref, kb_ref, vb_ref, q_ref, gd_ref, qr_ref, kr_ref, vr_ref, gr_ref):
    tm = x_ref.shape[0]
    x = x_ref[...]
    inv = lax.rsqrt(jnp.mean(x * x, axis=-1, keepdims=True) + EPS)
    h = (x * inv * g_ref[...]).astype(BF16)

    def proj(i):
        return jnp.dot(h, w_ref[:, _OFF[i]:_OFF[i + 1]], preferred_element_type=F32)

    def store_heads_as_rows(ref, z):
        for hd in range(H_D):
            ref[pl.ds(hd, tm, stride=H_D), :] = z[:, hd * LANES:(hd + 1) * LANES]

    q_ref[...] = (proj(0) * Q_PRESCALE).astype(BF16)
    zk = proj(1)
    store_heads_as_rows(k_ref, zk)
    kb_ref[...] = zk.astype(BF16)
    zv = proj(2)
    store_heads_as_rows(v_ref, zv)
    vb_ref[...] = zv.astype(BF16)
    gd_ref[...] = proj(3).astype(BF16)

    cos = cos_ref[...]
    sin = sin_ref[...]
    lane = lax.broadcasted_iota(jnp.int32, cos.shape, 1)
    low_half = (lane % DK_R) < (DK_R // 2)

    def rotate(z):
        outs = []
        for c in range(z.shape[1] // LANES):
            zc = z[:, c * LANES:(c + 1) * LANES]
            partner = jnp.where(low_half,
                                pltpu.roll(zc, LANES - DK_R // 2, 1),
                                pltpu.roll(zc, DK_R // 2, 1))
            outs.append(zc * cos + partner * sin)
        return jnp.concatenate(outs, axis=1)

    qr_ref[...] = rotate(proj(4)).astype(BF16)
    kr_ref[...] = (rotate(proj(5)) * (DK_R ** -0.5)).astype(BF16)
    vr_ref[...] = proj(6).astype(BF16)
    gr_ref[...] = proj(7).astype(BF16)


def _inproj_specs(x2d, w_bf16, cos_t, tm):
    n, d = x2d.shape
    n_tab = cos_t.shape[0] // tm
    row = lambda i: (i, 0)
    tab = lambda i: (i % n_tab, 0)
    outs = [(H_D, 2 * D_QK, F32), (H_D, DV_D, F32), (1, W_D, BF16), (1, W_D, BF16),
            (1, 2 * H_D * D_QK, BF16), (1, W_D, BF16),
            (1, H_R * DK_R, BF16), (1, H_R * DK_R, BF16), (1, W_R, BF16), (1, W_R, BF16)]
    in_specs = [pl.BlockSpec((tm, d), row),
                pl.BlockSpec((1, d), lambda i: (0, 0)),
                pl.BlockSpec(w_bf16.shape, lambda i: (0, 0), pipeline_mode=pl.Buffered(1)),
                pl.BlockSpec((tm, LANES), tab),
                pl.BlockSpec((tm, LANES), tab)]
    out_specs = [pl.BlockSpec((tm * r, w), row) for r, w, _ in outs]
    out_shape = [jax.ShapeDtypeStruct((n * r, w), dt) for r, w, dt in outs]
    return in_specs, out_specs, out_shape


def _in_projection(x2d, g_pre, w_bf16, cos_t, sin_t, tm):
    in_specs, out_specs, out_shape = _inproj_specs(x2d, w_bf16, cos_t, tm)
    return pl.pallas_call(
        _inproj_kernel,
        grid=(x2d.shape[0] // tm,),
        in_specs=in_specs,
        out_specs=out_specs,
        out_shape=out_shape,
        compiler_params=_cparams(1),
        name="in_projection",
    )(x2d, g_pre, w_bf16, cos_t, sin_t)


def _attn_kernel(lam_ref, qi_ref, kj_ref, q_ref, kb, v_ref, tab_ref, o_ref, vb, s_bufs, mx_bufs, m_all, acc_all,
                 *, t, n_pairs, unroll):
    n_buf = s_bufs.shape[0]
    assert unroll % n_buf == 0
    s_len = kb.shape[0]
    nq = s_len // t

    @pl.when((pl.program_id(0) == 0) & (pl.program_id(1) == 0))
    def _():
        m_all[...] = jnp.full(m_all.shape, NEG, F32)
        acc_all[...] = jnp.zeros(acc_all.shape, F32)

    vb[:, :DV_D] = v_ref[...]
    vb[:, DV_D:] = jnp.ones((s_len, DV_D), BF16)
    lane = lax.broadcasted_iota(jnp.int32, (t, 2 * D_QK), 1)

    def scores_into(slot, n):
        q = q_ref[pl.ds(pl.multiple_of(qi_ref[n] * t, t), t), :]
        zero = jnp.zeros_like(q)
        qs = jnp.concatenate([jnp.where(lane < D_QK, q, zero), jnp.where(lane >= D_QK, q, zero)], axis=0)
        keys = kb[pl.ds(pl.multiple_of(kj_ref[n] * t, t), t), :]
        tab = tab_ref[jnp.clip(kj_ref[n] - qi_ref[n] + 2, 0, 2), 0]
        s = (lax.dot_general(qs, keys, (((1,), (1,)), ((), ())), preferred_element_type=F32)
             + jnp.concatenate([tab, tab], axis=0))
        s_bufs[slot] = s
        mx = s[:, :LANES]
        for c in range(1, t // LANES):
            mx = jnp.maximum(mx, s[:, c * LANES:(c + 1) * LANES])
        mx_bufs[slot] = mx

    def consume(slot, n):
        i = qi_ref[n]
        j = kj_ref[n]
        m_prev = m_all[i]
        m_new = jnp.maximum(m_prev, jnp.max(mx_bufs[slot], axis=1, keepdims=True))
        corr = jnp.exp2(m_prev - m_new)
        p = jnp.exp2(s_bufs[slot] - jnp.concatenate([m_new] * (t // LANES), axis=1))
        pv = jnp.dot(p.astype(BF16), vb[pl.ds(pl.multiple_of(j * t, t), t), :], preferred_element_type=F32)
        acc_all[i] = acc_all[i] * jnp.concatenate([corr, corr], axis=1) + pv
        m_all[i] = m_new

    scores_into(0, 0)

    def sweep(h, carry):
        n = unroll * h
        for u in range(unroll):
            scores_into((u + 1) % n_buf, n + u + 1)
            consume(u % n_buf, n + u)
        return carry

    lax.fori_loop(0, n_pairs // unroll, sweep, 0)
    for n in range(n_pairs - n_pairs % unroll, n_pairs):
        scores_into((n + 1) % n_buf, n + 1)
        consume(n % n_buf, n)

    for i in range(nq):
        acc = acc_all[i]
        nrm = acc[:, :DV_D] / acc[:, DV_D:]
        o_ref[i * t:(i + 1) * t, :] = (nrm[:t] - lam_ref[0] * nrm[t:]).astype(o_ref.dtype)
        m_all[i] = jnp.full(m_all.shape[1:], NEG, F32)


def _prompt_attention(lam, q2d, k2d, v2d, tabs, batch, seq, t, n_buf=3, unroll=6):
    nq = seq // t
    pairs = [(i, j) for i in range(nq) for j in range(i + 1)]
    pairs.append(pairs[-1])
    qi = jnp.asarray([p[0] for p in pairs], jnp.int32)
    kj = jnp.asarray([p[1] for p in pairs], jnp.int32)
    bh = lambda b, h: (b, h)
    smem = pl.BlockSpec(memory_space=pltpu.SMEM)
    return pl.pallas_call(
        functools.partial(_attn_kernel, t=t, n_pairs=len(pairs) - 1, unroll=unroll),
        grid=(batch, H_D),
        in_specs=[smem, smem, smem,
                  pl.BlockSpec((seq, 2 * D_QK), bh),
                  pl.BlockSpec((seq, 2 * D_QK), bh),
                  pl.BlockSpec((seq, DV_D), bh),
                  pl.BlockSpec((3, 1, t, t), lambda b, h: (0, h, 0, 0))],
        out_specs=pl.BlockSpec((seq, DV_D), bh),
        out_shape=jax.ShapeDtypeStruct((batch * seq, W_D), BF16),
        scratch_shapes=[pltpu.VMEM((seq, 2 * DV_D), BF16),
                        pltpu.VMEM((n_buf, 2 * t, t), F32),
                        pltpu.VMEM((n_buf, 2 * t, LANES), F32),
                        pltpu.VMEM((nq, 2 * t, LANES), F32),
                        pltpu.VMEM((nq, 2 * t, 2 * DV_D), F32)],
        compiler_params=_cparams(2),
        name="prompt_attention",
    )(lam, qi, kj, q2d, k2d, v2d, tabs)


def _ret_kernel(qr_ref, kr_ref, vr_ref, dec_ref, qdec_ref, kdec_ref, o_ref, s_out_ref, st_sc, *, g_chunk, bb):
    c = pl.program_id(1)
    chunk = qr_ref.shape[1]

    @pl.when(c == 0)
    def _():
        st_sc[...] = jnp.zeros(st_sc.shape, F32)

    head_of_lane = lax.broadcasted_iota(jnp.int32, (chunk, H_R * DK_R), 1) // DK_R
    for i in range(bb):
        q = qr_ref[i]
        k = kr_ref[i]
        v = vr_ref[i]
        st = st_sc[i]
        q_all = jnp.concatenate([jnp.where(head_of_lane == h, q, jnp.zeros_like(q)) for h in range(H_R)], axis=0)
        sc_all = lax.dot_general(q_all, k, (((1,), (1,)), ((), ())), preferred_element_type=F32) * dec_ref[...]
        cross_all = jnp.dot(q_all, st.astype(BF16), preferred_element_type=F32) * qdec_ref[...]
        kd_t = (k.astype(F32) * kdec_ref[...]).T
        for h in range(H_R):
            tok = slice(h * chunk, (h + 1) * chunk)
            rows = slice(h * DK_R, (h + 1) * DK_R)
            vh = v[:, h * DV_R:(h + 1) * DV_R]
            inner = jnp.dot(sc_all[tok].astype(BF16), vh, preferred_element_type=F32)
            o_ref[i, :, h * DV_R:(h + 1) * DV_R] = (inner + cross_all[tok]).astype(o_ref.dtype)
            upd = jnp.dot(kd_t[rows, :].astype(BF16), vh, preferred_element_type=F32)
            st_new = g_chunk[h] * st[rows, :] + upd
            st_sc[i, rows, :] = st_new
            s_out_ref[i, h] = st_new


def _prompt_retention_merge(qr, kr, vr, x2d, od, gd, gr, gn_d, gn_r, w_bf16, g_post, batch, seq, bb=8,
                            chain_rows=512):
    c = min(RET_CHUNK, seq)
    nc = seq // c
    bb = math.gcd(bb, batch)
    rows_per_chain = math.gcd(max(1, chain_rows // c), bb)
    d_model = x2d.shape[1]
    lg = _log_gamma_np()
    idx = np.arange(c, dtype=np.float64)
    d = idx[:, None] - idx[None, :]
    decay = np.where(d[None] >= 0, np.exp(lg[:, None, None] * np.maximum(d, 0.0)[None]), 0.0)
    qdec = np.broadcast_to(np.exp(lg[:, None] * (idx + 1.0)[None, :])[:, :, None], (H_R, c, DV_R))
    kdec = np.repeat(np.exp(lg[:, None] * (c - 1.0 - idx)[None, :]).T, DK_R, axis=1)
    g_chunk = tuple(float(v) for v in np.exp(lg * c))
    blk = lambda w: pl.BlockSpec((bb, c, w), lambda b, j: (b, j, 0))
    fix = lambda b, j: (0, 0)
    as3 = lambda a: a.reshape(batch, seq, a.shape[-1])
    y, s = pl.pallas_call(
        functools.partial(_ret_merge_kernel, g_chunk=g_chunk, bb=bb, rows_per_chain=rows_per_chain),
        grid=(batch // bb, nc),
        in_specs=[blk(H_R * DK_R), blk(H_R * DK_R), blk(W_R),
                  pl.BlockSpec((H_R * c, c), fix),
                  pl.BlockSpec((H_R * c, DV_R), fix),
                  pl.BlockSpec((c, H_R * DK_R), fix),
                  blk(d_model), blk(W_D), blk(W_D), blk(W_R),
                  pl.BlockSpec((1, DV_D), fix),
                  pl.BlockSpec((1, DV_R), fix),
                  pl.BlockSpec(w_bf16.shape, fix),
                  pl.BlockSpec((1, d_model), fix)],
        out_specs=[blk(d_model),
                   pl.BlockSpec((bb, H_R, DK_R, DV_R), lambda b, j: (b, 0, 0, 0))],
        out_shape=[jax.ShapeDtypeStruct((batch, seq, d_model), F32),
                   jax.ShapeDtypeStruct((batch, H_R, DK_R, DV_R), F32)],
        scratch_shapes=[pltpu.VMEM((bb, H_R * DK_R, DV_R), F32),
                        pltpu.VMEM((bb, c, W_R), F32)],
        compiler_params=_cparams(2),
        name="retention_merge",
    )(as3(qr), as3(kr), as3(vr), jnp.asarray(decay.reshape(H_R * c, c), F32),
      jnp.asarray(qdec.reshape(H_R * c, DV_R), F32), jnp.asarray(kdec, F32),
      as3(x2d), as3(od), as3(gd), as3(gr), gn_d, gn_r, w_bf16, g_post)
    return y.reshape(batch * seq, d_model), s


def _decode_row(b, r, pt_ref, lam_ref, q_ref, kn_ref, vn_ref, tab_ref, ck_hbm, cv_hbm, o_ref,
                kbuf, vbuf, knew, vnew, sem, *, n_batch, n_pages, group, n_slot, t_new):
    n_group = n_pages // group
    rows = H_D * 2 * t_new
    width = kbuf.shape[2]

    def page_copies(bb, g, slot, real):
        out = []
        for p in range(group):
            phys = pt_ref[bb * n_pages + g * group + p] if real else 0
            out.append(pltpu.make_async_copy(ck_hbm.at[phys], kbuf.at[slot, p], sem.at[0, slot]))
            out.append(pltpu.make_async_copy(cv_hbm.at[phys], vbuf.at[slot, p], sem.at[1, slot]))
        return out

    def start_group(bb, g, slot):
        for cp in page_copies(bb, g, slot, True):
            cp.start()

    @pl.when(b == 0)
    def _():
        for g0 in range(n_slot):
            start_group(0, g0, g0)
        knew[...] = jnp.zeros(knew.shape, F32)
        vnew[...] = jnp.zeros(vnew.shape, F32)

    row = lax.broadcasted_iota(jnp.int32, (rows, 2 * D_QK), 0)
    lane = lax.broadcasted_iota(jnp.int32, (rows, 2 * D_QK), 1)
    own_half = ((row // t_new) % 2 == 0) == (lane < D_QK)
    q = jnp.where(own_half, q_ref[r], 0.0).astype(BF16)

    def softmax_update(state, scores, values):
        m, l, acc = state
        mx = scores[0]
        for s in scores[1:]:
            mx = jnp.maximum(mx, s)
        m_new = jnp.maximum(m, jnp.broadcast_to(jnp.max(mx, axis=1, keepdims=True), m.shape))
        corr = jnp.exp2(m - m_new)
        m_wide = jnp.concatenate([m_new] * (width // LANES), axis=1)
        ps = [jnp.exp2(s - m_wide) for s in scores]
        psum = ps[0]
        for p in ps[1:]:
            psum = psum + p
        l_new = l * corr + jnp.broadcast_to(jnp.sum(psum, axis=1, keepdims=True), l.shape)
        pv = jnp.dot(ps[0].astype(BF16), values[0], preferred_element_type=F32)
        for p, vv in zip(ps[1:], values[1:]):
            pv = pv + jnp.dot(p.astype(BF16), vv, preferred_element_type=F32)
        return m_new, l_new, acc * corr + pv

    def qk(keys):
        return lax.dot_general(q, keys, (((1,), (1,)), ((), ())), preferred_element_type=F32)

    state = (jnp.full((rows, LANES), NEG, F32), jnp.zeros((rows, LANES), F32), jnp.zeros((rows, DV_D), F32))
    for g in range(n_group):
        slot = g % n_slot if n_group % n_slot == 0 else lax.rem(b * n_group + g, n_slot)
        for cp in page_copies(b, g, slot, False):
            cp.wait()
        tabs = [tab_ref[0]] * (group - 1) + [tab_ref[1 if g == n_group - 1 else 0]]
        scores = [qk(kbuf[slot, p].astype(BF16)) + tabs[p] for p in range(group)]
        values = [vbuf[slot, p].astype(BF16) for p in range(group)]
        state = softmax_update(state, scores, values)
        if g + n_slot < n_group:
            start_group(b, g + n_slot, slot)
        else:
            @pl.when(b + 1 < n_batch)
            def _():
                start_group(b + 1, g + n_slot - n_group, slot)

    n_new = kn_ref.shape[1]
    knew[0:n_new, :] = kn_ref[r]
    vnew[0:n_new, :] = vn_ref[r]
    s_new = qk(knew[...].astype(BF16)) + tab_ref[2]
    _, l, acc = softmax_update(state, [s_new], [vnew[...].astype(BF16)])
    n = acc / l
    o_ref[r] = n - lam_ref[0] * pltpu.roll(n, rows - t_new, 0)


def _inproj_decode_kernel(*refs, rows_per_step, **cfg):
    pt_ref, lam_ref = refs[:2]
    k0 = 2 + _N_INPROJ_IN
    dec_in = refs[k0:k0 + 6]
    k1 = k0 + 6
    o_ref = refs[k1 + _N_INPROJ_OUT]
    scratch = refs[k1 + _N_INPROJ_OUT + 1:]
    _inproj_kernel(*refs[2:k0], *refs[k1:k1 + _N_INPROJ_OUT])
    for r in range(rows_per_step):
        b = pl.program_id(0) * rows_per_step + r

        @pl.when(b < cfg["n_batch"])
        def _():
            _decode_row(b, r, pt_ref, lam_ref, *dec_in, o_ref, *scratch, **cfg)


def _in_projection_with_decode(x2d, g_pre, w_bf16, cos_t, sin_t, tm,
                               page_table, lam, q_rows, k_new, v_new, tabs, cache_k, cache_v,
                               group=8, n_slot=6):
    n_steps = x2d.shape[0] // tm
    n_batch, rows, _ = q_rows.shape
    n_pages = page_table.shape[1]
    width = cache_k.shape[1]
    group = math.gcd(group, n_pages)
    n_slot = min(n_slot, n_pages // group)
    rows_per_step = -(-n_batch // n_steps)
    n_blocks = n_batch // rows_per_step
    assert n_blocks * rows_per_step == n_batch, "sample rows must split evenly over the grid steps"
    per_b = lambda a: pl.BlockSpec((rows_per_step,) + a.shape[1:],
                                   lambda i: (jnp.minimum(i, n_blocks - 1), 0, 0))
    in_specs, out_specs, out_shape = _inproj_specs(x2d, w_bf16, cos_t, tm)
    smem = pl.BlockSpec(memory_space=pltpu.SMEM)
    res = pl.pallas_call(
        functools.partial(_inproj_decode_kernel, rows_per_step=rows_per_step, n_batch=n_batch,
                          n_pages=n_pages, group=group, n_slot=n_slot, t_new=rows // (2 * H_D)),
        grid=(n_steps,),
        in_specs=[smem, smem] + in_specs + [per_b(q_rows), per_b(k_new), per_b(v_new),
                                             pl.BlockSpec(tabs.shape, lambda i: (0, 0, 0)),
                                             pl.BlockSpec(memory_space=pl.ANY),
                                             pl.BlockSpec(memory_space=pl.ANY)],
        out_specs=out_specs + [per_b(q_rows)],
        out_shape=out_shape + [jax.ShapeDtypeStruct(q_rows.shape, F32)],
        scratch_shapes=[pltpu.VMEM((n_slot, group, width, LANES), F32),
                        pltpu.VMEM((n_slot, group, width, LANES), F32),
                        pltpu.VMEM((width, LANES), F32),
                        pltpu.VMEM((width, LANES), F32),
                        pltpu.SemaphoreType.DMA((2, n_slot))],
        compiler_params=_cparams(1),
        name="in_projection_with_decode",
    )(page_table.reshape(-1), lam, x2d, g_pre, w_bf16, cos_t, sin_t, q_rows, k_new, v_new, tabs,
      cache_k, cache_v)
    return res[:_N_INPROJ_OUT], res[_N_INPROJ_OUT]


def _sample_ret_kernel(qt_ref, kt_ref, v_ref, s_ref, o_ref, s_out_ref, *, gamma, bb, t_new):
    for i in range(bb):
        v = v_ref[i]
        for h in range(H_R):
            st = s_ref[i, h]
            qt = qt_ref[i, h]
            kt = kt_ref[i, h]
            outs = []
            for t in range(t_new):
                vrow = v[t:t + 1, h * DV_R:(h + 1) * DV_R]
                st = gamma[h] * st + kt[:, t:t + 1] * vrow
                outs.append(jnp.sum(qt[:, t:t + 1] * st, axis=0, keepdims=True))
            o_ref[i, :, h * DV_R:(h + 1) * DV_R] = jnp.concatenate(outs, axis=0)
            s_out_ref[i, h] = st


def _sample_retention(qr_t, kr_t, vr, state, bb=8):
    n_batch, _, _, t_new = qr_t.shape
    bb = math.gcd(bb, n_batch)
    gamma = tuple(float(v) for v in np.exp(_log_gamma_np()))
    m4 = lambda i: (i, 0, 0, 0)
    m3 = lambda i: (i, 0, 0)
    return pl.pallas_call(
        functools.partial(_sample_ret_kernel, gamma=gamma, bb=bb, t_new=t_new),
        grid=(n_batch // bb,),
        in_specs=[pl.BlockSpec((bb, H_R, DK_R, t_new), m4),
                  pl.BlockSpec((bb, H_R, DK_R, t_new), m4),
                  pl.BlockSpec((bb, t_new, W_R), m3),
                  pl.BlockSpec((bb, H_R, DK_R, DV_R), m4)],
        out_specs=[pl.BlockSpec((bb, t_new, W_R), m3),
                   pl.BlockSpec((bb, H_R, DK_R, DV_R), m4)],
        out_shape=[jax.ShapeDtypeStruct((n_batch, t_new, W_R), F32),
                   jax.ShapeDtypeStruct(state.shape, F32)],
        compiler_params=_cparams(1),
        name="sample_retention",
    )(qr_t, kr_t, vr, state)


def _merge_rows(x, od, orr, gd, gr, gn_d, gn_r, w, g_post):
    def gated(o_all, g_all, gn):
        parts = []
        for h in range(o_all.shape[1] // LANES):
            cols = slice(h * LANES, (h + 1) * LANES)
            o = o_all[:, cols].astype(F32)
            g = g_all[:, cols].astype(F32)
            y = o * lax.rsqrt(jnp.mean(o * o, axis=-1, keepdims=True) + EPS) * gn
            parts.append((y * (g * jax.nn.sigmoid(g))).astype(BF16))
        return parts

    o = jnp.concatenate(gated(od, gd, gn_d) + gated(orr, gr, gn_r), axis=1)
    mix = jnp.dot(o, w, preferred_element_type=F32)
    return x + mix * lax.rsqrt(jnp.mean(mix * mix, axis=-1, keepdims=True) + EPS) * g_post


def _merge_kernel(x_ref, od_ref, or_ref, gd_ref, gr_ref, gnd_ref, gnr_ref, w_ref, gpost_ref, y_ref, *, chain):
    for c in range(x_ref.shape[0] // chain):
        rows = slice(c * chain, (c + 1) * chain)
        y_ref[rows, :] = _merge_rows(x_ref[rows, :], od_ref[rows, :], or_ref[rows, :], gd_ref[rows, :],
                                     gr_ref[rows, :], gnd_ref[...], gnr_ref[...], w_ref[...], gpost_ref[...])


def _ret_merge_kernel(qr_ref, kr_ref, vr_ref, dec_ref, qdec_ref, kdec_ref,
                      x_ref, od_ref, gd_ref, gr_ref, gnd_ref, gnr_ref, w_ref, gpost_ref,
                      y_ref, s_out_ref, st_sc, or_sc, *, g_chunk, bb, rows_per_chain):
    _ret_kernel(qr_ref, kr_ref, vr_ref, dec_ref, qdec_ref, kdec_ref, or_sc, s_out_ref, st_sc,
                g_chunk=g_chunk, bb=bb)
    chunk = x_ref.shape[1]
    for c0 in range(0, bb, rows_per_chain):
        rows = range(c0, c0 + rows_per_chain)
        cat = lambda ref: jnp.concatenate([ref[i] for i in rows], axis=0)
        y = _merge_rows(cat(x_ref), cat(od_ref), cat(or_sc), cat(gd_ref), cat(gr_ref),
                        gnd_ref[...], gnr_ref[...], w_ref[...], gpost_ref[...])
        for k, i in enumerate(rows):
            y_ref[i] = y[k * chunk:(k + 1) * chunk]


def _merge(x2d, od, orr, gd, gr, gn_d, gn_r, w_bf16, g_post, chain, n_chain=2):
    n, d = x2d.shape
    tm = chain * math.gcd(n_chain, n // chain)
    row = lambda i: (i, 0)
    fix = lambda i: (0, 0)
    return pl.pallas_call(
        functools.partial(_merge_kernel, chain=chain),
        grid=(n // tm,),
        in_specs=[pl.BlockSpec((tm, d), row),
                  pl.BlockSpec((tm, W_D), row),
                  pl.BlockSpec((tm, W_R), row),
                  pl.BlockSpec((tm, W_D), row),
                  pl.BlockSpec((tm, W_R), row),
                  pl.BlockSpec((1, DV_D), fix),
                  pl.BlockSpec((1, DV_R), fix),
                  pl.BlockSpec(w_bf16.shape, fix),
                  pl.BlockSpec((1, d), fix)],
        out_specs=pl.BlockSpec((tm, d), row),
        out_shape=jax.ShapeDtypeStruct((n, d), F32),
        compiler_params=_cparams(1),
        name="merge_out_projection",
    )(x2d, od, orr, gd, gr, gn_d, gn_r, w_bf16, g_post)


def _pick_tile(n, target):
    t = math.gcd(n, target)
    assert t % 8 == 0, (n, target)
    return t


def kernel(x_prompt, x_sample, cache_k, cache_v, state_ret, page_table, rel_bias, norm_pre, norm_post,
           w_in, w_out, lambda_q1, lambda_k1, lambda_q2, lambda_k2, gn_diff, gn_ret):
    depth = w_in.shape[0]
    batch, seq, d_model = x_prompt.shape
    dec_batch, t_new, _ = x_sample.shape
    n_pool, page = cache_k.shape[1], cache_k.shape[2]
    n_pages = page_table.shape[1]
    past = n_pages * page
    far = _far_distance()

    t_attn = _pick_tile(seq, 512)
    assert t_attn > far and page > far, "bias must be constant beyond the neighbouring tile / page"
    tm_p = _pick_tile(batch * seq, 512)
    assert seq % tm_p == 0
    tm_s = _pick_tile(dec_batch * t_new, 512)

    vals = jnp.concatenate([(rel_bias - rel_bias[NUM_BUCKETS - 1:]) * LOG2E,
                            jnp.full((1, H_D), NEG, F32)], axis=0).astype(F32)
    tabs_p = _bias_tables(vals, jnp.asarray(_prompt_bucket_codes(t_attn)))
    tabs_s = _bias_tables(vals, jnp.asarray(_decode_bucket_codes(page, t_new)))

    cos_p, sin_p = _rotation_tables(jnp.arange(seq, dtype=jnp.int32))
    pos_s = past + (jnp.arange(tm_s, dtype=jnp.int32) % t_new)
    cos_s, sin_s = _rotation_tables(pos_s)

    yp = x_prompt.reshape(batch * seq, d_model)
    ys = x_sample.reshape(dec_batch * t_new, d_model)
    kp_l, vp_l, sp_l, ks_l, vs_l, ss_l = [], [], [], [], [], []
    for layer in range(depth):
        lambda_init = 0.8 - 0.6 * math.exp(-0.3 * layer)
        lam = (jnp.exp(jnp.sum(lambda_q1[layer].astype(F32) * lambda_k1[layer].astype(F32)))
               - jnp.exp(jnp.sum(lambda_q2[layer].astype(F32) * lambda_k2[layer].astype(F32)))
               + lambda_init).reshape(1).astype(F32)
        w_in_b = w_in[layer].astype(BF16)
        w_out_b = w_out[layer].astype(BF16)
        g_pre = norm_pre[layer].reshape(1, d_model)
        g_post = norm_post[layer].reshape(1, d_model)
        gn_d = (gn_diff[layer].astype(F32) * (1.0 - lambda_init)).reshape(1, DV_D)
        gn_r = gn_ret[layer].astype(F32).reshape(1, DV_R)

        k_s, v_s, _, _, q_s, gd_s, qr_s, kr_s, vr_s, gr_s = _in_projection(
            ys, g_pre, w_in_b, cos_s, sin_s, tm_s)
        q4 = q_s.astype(F32).reshape(dec_batch, t_new, H_D, 2 * D_QK).transpose(0, 2, 1, 3)
        q_rows = jnp.concatenate([q4, q4], axis=2).reshape(dec_batch, H_D * 2 * t_new, 2 * D_QK)

        (k_p, v_p, kb_p, vb_p, q_p, gd_p, qr_p, kr_p, vr_p, gr_p), od_s = _in_projection_with_decode(
            yp, g_pre, w_in_b, cos_p, sin_p, tm_p,
            page_table + layer * n_pool, lam, q_rows,
            k_s.reshape(dec_batch, t_new * H_D, 2 * D_QK),
            v_s.reshape(dec_batch, t_new * H_D, DV_D),
            tabs_s.reshape(3, H_D * 2 * t_new, page * H_D),
            cache_k.reshape(depth * n_pool, page * H_D, 2 * D_QK),
            cache_v.reshape(depth * n_pool, page * H_D, DV_D))
        od_p = _prompt_attention(lam, q_p, kb_p, vb_p, tabs_p, batch, seq, t_attn)
        yp, s_p = _prompt_retention_merge(qr_p, kr_p, vr_p, yp, od_p, gd_p, gr_p, gn_d, gn_r, w_out_b, g_post,
                                          batch, seq)
        kp_l.append(k_p.reshape(batch, seq, H_D, 2 * D_QK))
        vp_l.append(v_p.reshape(batch, seq, H_D, DV_D))
        sp_l.append(s_p)

        od_s = od_s.reshape(dec_batch, H_D, 2, t_new, DV_D)[:, :, 0].transpose(0, 2, 1, 3)
        od_s = od_s.reshape(dec_batch * t_new, W_D)
        to_cols = lambda a: a.astype(F32).reshape(dec_batch, t_new, H_R, DK_R).transpose(0, 2, 3, 1)
        or_s, s_s = _sample_retention(to_cols(qr_s), to_cols(kr_s),
                                      vr_s.astype(F32).reshape(dec_batch, t_new, W_R),
                                      state_ret[layer].astype(F32))
        ys = _merge(ys, od_s, or_s.reshape(dec_batch * t_new, W_R), gd_s, gr_s, gn_d, gn_r, w_out_b, g_post, tm_s)
        ks_l.append(k_s.reshape(dec_batch, t_new, H_D, 2 * D_QK))
        vs_l.append(v_s.reshape(dec_batch, t_new, H_D, DV_D))
        ss_l.append(s_s)

    stack = lambda parts: parts[0][None] if len(parts) == 1 else jnp.stack(parts)
    return (yp.reshape(batch, seq, d_model), ys.reshape(dec_batch, t_new, d_model),
            stack(kp_l), stack(vp_l), stack(sp_l), stack(ks_l), stack(vs_l), stack(ss_l))
```

```python
import functools
import math

import jax
import jax.numpy as jnp
import numpy as np
from jax import lax
from jax.experimental import pallas as pl
from jax.experimental.pallas import tpu as pltpu

F32 = jnp.float32
BF16 = jnp.bfloat16

H_D = 4
D_QK = 64
DV_D = 128
W_D = H_D * DV_D
H_R = 4
DK_R = 64
DV_R = 128
W_R = H_R * DV_R
NUM_BUCKETS = 32
MAX_DISTANCE = 128
RET_CHUNK = 128
EPS = 1e-6
NEG = -1e30
LOG2E = 1.4426950408889634
Q_PRESCALE = D_QK ** -0.5 * LOG2E
MASK_CODE = NUM_BUCKETS
LANES = 128
VMEM_LIMIT = 56 * 1024 * 1024

_SIZES = [H_D * 2 * D_QK, H_D * 2 * D_QK, W_D, W_D, H_R * DK_R, H_R * DK_R, W_R, W_R]
_OFF = [0] + [int(v) for v in np.cumsum(_SIZES)]


def _cparams(n_axes):
    return pltpu.CompilerParams(dimension_semantics=("arbitrary",) * n_axes,
                                vmem_limit_bytes=VMEM_LIMIT)


def _bucket_np(rel):
    n = np.maximum(rel, 0)
    max_exact = NUM_BUCKETS // 2
    nf = np.maximum(n, max_exact).astype(np.float32)
    large = max_exact + (np.log(nf / np.float32(max_exact)) / np.float32(math.log(MAX_DISTANCE / max_exact))
                         * np.float32(NUM_BUCKETS - max_exact)).astype(np.int32)
    large = np.minimum(large, NUM_BUCKETS - 1)
    return np.where(n < max_exact, n, large).astype(np.int32)


def _far_distance():
    b = _bucket_np(np.arange(4 * MAX_DISTANCE))
    return int(np.max(np.nonzero(b != NUM_BUCKETS - 1)[0])) + 1


def _prompt_bucket_codes(t):
    r = np.arange(t)[:, None]
    c = np.arange(t)[None, :]
    far = np.full((t, t), NUM_BUCKETS - 1)
    left = _bucket_np(r - c + t)
    diag = np.where(r >= c, _bucket_np(r - c), MASK_CODE)
    return np.stack([far, left, diag]).astype(np.int32)[:, None]


def _decode_bucket_codes(page, t_new):
    h = np.arange(H_D)[:, None, None]
    t = np.tile(np.arange(t_new), 2)[None, :, None]
    col = np.arange(page * H_D)[None, None, :]
    j, hk = col // H_D, col % H_D
    own = hk == h
    far = np.where(own, NUM_BUCKETS - 1, MASK_CODE) + 0 * t
    last = np.where(own, _bucket_np(page + t - j), MASK_CODE)
    new = np.where(own & (j <= t) & (j < t_new), _bucket_np(t - j), MASK_CODE)
    return np.stack([far, last, new]).astype(np.int32)


def _rotation_tables(pos):
    half = DK_R // 2
    inv = 1.0 / (10000.0 ** jnp.linspace(0.0, 1.0, half, dtype=F32))
    ang = pos.astype(F32)[:, None] * inv[None, :]
    cos = jnp.cos(ang)
    sin = jnp.sin(ang)
    cos_t = jnp.tile(cos, (1, LANES // half))
    sin_t = jnp.tile(jnp.concatenate([-sin, sin], axis=1), (1, LANES // DK_R))
    return cos_t, sin_t


def _log_gamma_np():
    return np.log(1.0 - 2.0 ** (-5.0 - np.arange(H_R, dtype=np.float64)))


def _table_kernel(vals_ref, code_ref, out_ref):
    h = pl.program_id(1)
    code = code_ref[0, 0]
    acc = jnp.full(code.shape, vals_ref[MASK_CODE, h], F32)
    for b in range(NUM_BUCKETS):
        acc = jnp.where(code == b, vals_ref[b, h], acc)
    out_ref[0, 0] = acc


def _bias_tables(vals, codes):
    n, hc, r, c = codes.shape
    code_map = (lambda t, h: (t, h, 0, 0)) if hc == H_D else (lambda t, h: (t, 0, 0, 0))
    return pl.pallas_call(
        _table_kernel,
        grid=(n, H_D),
        in_specs=[pl.BlockSpec(memory_space=pltpu.SMEM),
                  pl.BlockSpec((1, 1, r, c), code_map)],
        out_specs=pl.BlockSpec((1, 1, r, c), lambda t, h: (t, h, 0, 0)),
        out_shape=jax.ShapeDtypeStruct((n, H_D, r, c), F32),
        compiler_params=_cparams(2),
        name="bias_tables",
    )(vals, codes)


_N_INPROJ_IN = 5
_N_INPROJ_OUT = 10


def _inproj_kernel(x_ref, g_ref, w_ref, cos_ref, sin_ref,
                   k_ref, v_ref, kb_ref, vb_ref, q_ref, gd_ref, qr_ref, kr_ref, vr_ref, gr_ref):
    tm = x_ref.shape[0]
    x = x_ref[...]
    inv = lax.rsqrt(jnp.mean(x * x, axis=-1, keepdims=True) + EPS)
    h = (x * inv * g_ref[...]).astype(BF16)

    def proj(i):
        return jnp.dot(h, w_ref[:, _OFF[i]:_OFF[i + 1]], preferred_element_type=F32)

    def store_heads_as_rows(ref, z):
        for hd in range(H_D):
            ref[pl.ds(hd, tm, stride=H_D), :] = z[:, hd * LANES:(hd + 1) * LANES]

    q_ref[...] = (proj(0) * Q_PRESCALE).astype(BF16)
    zk = proj(1)
    store_heads_as_rows(k_ref, zk)
    kb_ref[...] = zk.astype(BF16)
    zv = proj(2)
    store_heads_as_rows(v_ref, zv)
    vb_ref[...] = zv.astype(BF16)
    gd_ref[...] = proj(3).astype(BF16)

    cos = cos_ref[...]
    sin = sin_ref[...]
    lane = lax.broadcasted_iota(jnp.int32, cos.shape, 1)
    low_half = (lane % DK_R) < (DK_R // 2)

    def rotate(z):
        outs = []
        for c in range(z.shape[1] // LANES):
            zc = z[:, c * LANES:(c + 1) * LANES]
            partner = jnp.where(low_half,
                                pltpu.roll(zc, LANES - DK_R // 2, 1),
                                pltpu.roll(zc, DK_R // 2, 1))
            outs.append(zc * cos + partner * sin)
        return jnp.concatenate(outs, axis=1)

    qr_ref[...] = rotate(proj(4)).astype(BF16)
    kr_ref[...] = (rotate(proj(5)) * (DK_R ** -0.5)).astype(BF16)
    vr_ref[...] = proj(6).astype(BF16)
    gr_ref[...] = proj(7).astype(BF16)


def _inproj_specs(x2d, w_bf16, cos_t, tm):
    n, d = x2d.shape
    n_tab = cos_t.shape[0] // tm
    row = lambda i: (i, 0)
    tab = lambda i: (i % n_tab, 0)
    outs = [(H_D, 2 * D_QK, F32), (H_D, DV_D, F32), (1, W_D, BF16), (1, W_D, BF16),
            (1, 2 * H_D * D_QK, BF16), (1, W_D, BF16),
            (1, H_R * DK_R, BF16), (1, H_R * DK_R, BF16), (1, W_R, BF16), (1, W_R, BF16)]
    in_specs = [pl.BlockSpec((tm, d), row),
                pl.BlockSpec((1, d), lambda i: (0, 0)),
                pl.BlockSpec(w_bf16.shape, lambda i: (0, 0), pipeline_mode=pl.Buffered(1)),
                pl.BlockSpec((tm, LANES), tab),
                pl.BlockSpec((tm, LANES), tab)]
    out_specs = [pl.BlockSpec((tm * r, w), row) for r, w, _ in outs]
    out_shape = [jax.ShapeDtypeStruct((n * r, w), dt) for r, w, dt in outs]
    return in_specs, out_specs, out_shape


def _in_projection(x2d, g_pre, w_bf16, cos_t, sin_t, tm):
    in_specs, out_specs, out_shape = _inproj_specs(x2d, w_bf16, cos_t, tm)
    return pl.pallas_call(
        _inproj_kernel,
        grid=(x2d.shape[0] // tm,),
        in_specs=in_specs,
        out_specs=out_specs,
        out_shape=out_shape,
        compiler_params=_cparams(1),
        name="in_projection",
    )(x2d, g_pre, w_bf16, cos_t, sin_t)


def _attn_kernel(lam_ref, qi_ref, kj_ref, q_ref, kb, v_ref, tab_ref, o_ref, vb, s_bufs, mx_bufs, m_all, acc_all,
                 *, t, n_pairs, unroll):
    n_buf = s_bufs.shape[0]
    assert unroll % n_buf == 0
    s_len = kb.shape[0]
    nq = s_len // t

    @pl.when((pl.program_id(0) == 0) & (pl.program_id(1) == 0))
    def _():
        m_all[...] = jnp.full(m_all.shape, NEG, F32)
        acc_all[...] = jnp.zeros(acc_all.shape, F32)

    vb[:, :DV_D] = v_ref[...]
    vb[:, DV_D:] = jnp.ones((s_len, DV_D), BF16)
    lane = lax.broadcasted_iota(jnp.int32, (t, 2 * D_QK), 1)

    def scores_into(slot, n):
        q = q_ref[pl.ds(pl.multiple_of(qi_ref[n] * t, t), t), :]
        zero = jnp.zeros_like(q)
        qs = jnp.concatenate([jnp.where(lane < D_QK, q, zero), jnp.where(lane >= D_QK, q, zero)], axis=0)
        keys = kb[pl.ds(pl.multiple_of(kj_ref[n] * t, t), t), :]
        tab = tab_ref[jnp.clip(kj_ref[n] - qi_ref[n] + 2, 0, 2), 0]
        s = (lax.dot_general(qs, keys, (((1,), (1,)), ((), ())), preferred_element_type=F32)
             + jnp.concatenate([tab, tab], axis=0))
        s_bufs[slot] = s
        mx = s[:, :LANES]
        for c in range(1, t // LANES):
            mx = jnp.maximum(mx, s[:, c * LANES:(c + 1) * LANES])
        mx_bufs[slot] = mx

    def consume(slot, n):
        i = qi_ref[n]
        j = kj_ref[n]
        m_prev = m_all[i]
        m_new = jnp.maximum(m_prev, jnp.max(mx_bufs[slot], axis=1, keepdims=True))
        corr = jnp.exp2(m_prev - m_new)
        p = jnp.exp2(s_bufs[slot] - jnp.concatenate([m_new] * (t // LANES), axis=1))
        pv = jnp.dot(p.astype(BF16), vb[pl.ds(pl.multiple_of(j * t, t), t), :], preferred_element_type=F32)
        acc_all[i] = acc_all[i] * jnp.concatenate([corr, corr], axis=1) + pv
        m_all[i] = m_new

    scores_into(0, 0)

    def sweep(h, carry):
        n = unroll * h
        for u in range(unroll):
            scores_into((u + 1) % n_buf, n + u + 1)
            consume(u % n_buf, n + u)
        return carry

    lax.fori_loop(0, n_pairs // unroll, sweep, 0)
    for n in range(n_pairs - n_pairs % unroll, n_pairs):
        scores_into((n + 1) % n_buf, n + 1)
        consume(n % n_buf, n)

    for i in range(nq):
        acc = acc_all[i]
        nrm = acc[:, :DV_D] / acc[:, DV_D:]
        o_ref[i * t:(i + 1) * t, :] = (nrm[:t] - lam_ref[0] * nrm[t:]).astype(o_ref.dtype)
        m_all[i] = jnp.full(m_all.shape[1:], NEG, F32)


def _prompt_attention(lam, q2d, k2d, v2d, tabs, batch, seq, t, n_buf=3, unroll=6):
    nq = seq // t
    pairs = [(i, j) for i in range(nq) for j in range(i + 1)]
    pairs.append(pairs[-1])
    qi = jnp.asarray([p[0] for p in pairs], jnp.int32)
    kj = jnp.asarray([p[1] for p in pairs], jnp.int32)
    bh = lambda b, h: (b, h)
    smem = pl.BlockSpec(memory_space=pltpu.SMEM)
    return pl.pallas_call(
        functools.partial(_attn_kernel, t=t, n_pairs=len(pairs) - 1, unroll=unroll),
        grid=(batch, H_D),
        in_specs=[smem, smem, smem,
                  pl.BlockSpec((seq, 2 * D_QK), bh),
                  pl.BlockSpec((seq, 2 * D_QK), bh),
                  pl.BlockSpec((seq, DV_D), bh),
                  pl.BlockSpec((3, 1, t, t), lambda b, h: (0, h, 0, 0))],
        out_specs=pl.BlockSpec((seq, DV_D), bh),
        out_shape=jax.ShapeDtypeStruct((batch * seq, W_D), BF16),
        scratch_shapes=[pltpu.VMEM((seq, 2 * DV_D), BF16),
                        pltpu.VMEM((n_buf, 2 * t, t), F32),
                        pltpu.VMEM((n_buf, 2 * t, LANES), F32),
                        pltpu.VMEM((nq, 2 * t, LANES), F32),
                        pltpu.VMEM((nq, 2 * t, 2 * DV_D), F32)],
        compiler_params=_cparams(2),
        name="prompt_attention",
    )(lam, qi, kj, q2d, k2d, v2d, tabs)


def _ret_kernel(qr_ref, kr_ref, vr_ref, dec_ref, qdec_ref, kdec_ref, o_ref, s_out_ref, st_sc, *, g_chunk, bb):
    c = pl.program_id(1)
    chunk = qr_ref.shape[1]

    @pl.when(c == 0)
    def _():
        st_sc[...] = jnp.zeros(st_sc.shape, F32)

    head_of_lane = lax.broadcasted_iota(jnp.int32, (chunk, H_R * DK_R), 1) // DK_R
    for i in range(bb):
        q = qr_ref[i]
        k = kr_ref[i]
        v = vr_ref[i]
        st = st_sc[i]
        q_all = jnp.concatenate([jnp.where(head_of_lane == h, q, jnp.zeros_like(q)) for h in range(H_R)], axis=0)
        sc_all = lax.dot_general(q_all, k, (((1,), (1,)), ((), ())), preferred_element_type=F32) * dec_ref[...]
        cross_all = jnp.dot(q_all, st.astype(BF16), preferred_element_type=F32) * qdec_ref[...]
        kd_t = (k.astype(F32) * kdec_ref[...]).T
        for h in range(H_R):
            tok = slice(h * chunk, (h + 1) * chunk)
            rows = slice(h * DK_R, (h + 1) * DK_R)
            vh = v[:, h * DV_R:(h + 1) * DV_R]
            inner = jnp.dot(sc_all[tok].astype(BF16), vh, preferred_element_type=F32)
            o_ref[i, :, h * DV_R:(h + 1) * DV_R] = (inner + cross_all[tok]).astype(o_ref.dtype)
            upd = jnp.dot(kd_t[rows, :].astype(BF16), vh, preferred_element_type=F32)
            st_new = g_chunk[h] * st[rows, :] + upd
            st_sc[i, rows, :] = st_new
            s_out_ref[i, h] = st_new


def _prompt_retention_merge(qr, kr, vr, x2d, od, gd, gr, gn_d, gn_r, w_bf16, g_post, batch, seq, bb=8,
                            chain_rows=512):
    c = min(RET_CHUNK, seq)
    nc = seq // c
    bb = math.gcd(bb, batch)
    rows_per_chain = math.gcd(max(1, chain_rows // c), bb)
    d_model = x2d.shape[1]
    lg = _log_gamma_np()
    idx = np.arange(c, dtype=np.float64)
    d = idx[:, None] - idx[None, :]
    decay = np.where(d[None] >= 0, np.exp(lg[:, None, None] * np.maximum(d, 0.0)[None]), 0.0)
    qdec = np.broadcast_to(np.exp(lg[:, None] * (idx + 1.0)[None, :])[:, :, None], (H_R, c, DV_R))
    kdec = np.repeat(np.exp(lg[:, None] * (c - 1.0 - idx)[None, :]).T, DK_R, axis=1)
    g_chunk = tuple(float(v) for v in np.exp(lg * c))
    blk = lambda w: pl.BlockSpec((bb, c, w), lambda b, j: (b, j, 0))
    fix = lambda b, j: (0, 0)
    as3 = lambda a: a.reshape(batch, seq, a.shape[-1])
    y, s = pl.pallas_call(
        functools.partial(_ret_merge_kernel, g_chunk=g_chunk, bb=bb, rows_per_chain=rows_per_chain),
        grid=(batch // bb, nc),
        in_specs=[blk(H_R * DK_R), blk(H_R * DK_R), blk(W_R),
                  pl.BlockSpec((H_R * c, c), fix),
                  pl.BlockSpec((H_R * c, DV_R), fix),
                  pl.BlockSpec((c, H_R * DK_R), fix),
                  blk(d_model), blk(W_D), blk(W_D), blk(W_R),
                  pl.BlockSpec((1, DV_D), fix),
                  pl.BlockSpec((1, DV_R), fix),
                  pl.BlockSpec(w_bf16.shape, fix),
                  pl.BlockSpec((1, d_model), fix)],
        out_specs=[blk(d_model),
                   pl.BlockSpec((bb, H_R, DK_R, DV_R), lambda b, j: (b, 0, 0, 0))],
        out_shape=[jax.ShapeDtypeStruct((batch, seq, d_model), F32),
                   jax.ShapeDtypeStruct((batch, H_R, DK_R, DV_R), F32)],
        scratch_shapes=[pltpu.VMEM((bb, H_R * DK_R, DV_R), F32),
                        pltpu.VMEM((bb, c, W_R), F32)],
        compiler_params=_cparams(2),
        name="retention_merge",
    )(as3(qr), as3(kr), as3(vr), jnp.asarray(decay.reshape(H_R * c, c), F32),
      jnp.asarray(qdec.reshape(H_R * c, DV_R), F32), jnp.asarray(kdec, F32),
      as3(x2d), as3(od), as3(gd), as3(gr), gn_d, gn_r, w_bf16, g_post)
    return y.reshape(batch * seq, d_model), s


def _decode_row(b, r, pt_ref, lam_ref, q_ref, kn_ref, vn_ref, tab_ref, ck_hbm, cv_hbm, o_ref,
                kbuf, vbuf, knew, vnew, sem, *, n_batch, n_pages, group, n_slot, t_new):
    n_group = n_pages // group
    rows = H_D * 2 * t_new
    width = kbuf.shape[2]

    def page_copies(bb, g, slot, real):
        out = []
        for p in range(group):
            phys = pt_ref[bb * n_pages + g * group + p] if real else 0
            out.append(pltpu.make_async_copy(ck_hbm.at[phys], kbuf.at[slot, p], sem.at[0, slot]))
            out.append(pltpu.make_async_copy(cv_hbm.at[phys], vbuf.at[slot, p], sem.at[1, slot]))
        return out

    def start_group(bb, g, slot):
        for cp in page_copies(bb, g, slot, True):
            cp.start()

    @pl.when(b == 0)
    def _():
        for g0 in range(n_slot):
            start_group(0, g0, g0)
        knew[...] = jnp.zeros(knew.shape, F32)
        vnew[...] = jnp.zeros(vnew.shape, F32)

    row = lax.broadcasted_iota(jnp.int32, (rows, 2 * D_QK), 0)
    lane = lax.broadcasted_iota(jnp.int32, (rows, 2 * D_QK), 1)
    own_half = ((row // t_new) % 2 == 0) == (lane < D_QK)
    q = jnp.where(own_half, q_ref[r], 0.0).astype(BF16)

    def softmax_update(state, scores, values):
        m, l, acc = state
        mx = scores[0]
        for s in scores[1:]:
            mx = jnp.maximum(mx, s)
        m_new = jnp.maximum(m, jnp.broadcast_to(jnp.max(mx, axis=1, keepdims=True), m.shape))
        corr = jnp.exp2(m - m_new)
        m_wide = jnp.concatenate([m_new] * (width // LANES), axis=1)
        ps = [jnp.exp2(s - m_wide) for s in scores]
        psum = ps[0]
        for p in ps[1:]:
            psum = psum + p
        l_new = l * corr + jnp.broadcast_to(jnp.sum(psum, axis=1, keepdims=True), l.shape)
        pv = jnp.dot(ps[0].astype(BF16), values[0], preferred_element_type=F32)
        for p, vv in zip(ps[1:], values[1:]):
            pv = pv + jnp.dot(p.astype(BF16), vv, preferred_element_type=F32)
        return m_new, l_new, acc * corr + pv

    def qk(keys):
        return lax.dot_general(q, keys, (((1,), (1,)), ((), ())), preferred_element_type=F32)

    state = (jnp.full((rows, LANES), NEG, F32), jnp.zeros((rows, LANES), F32), jnp.zeros((rows, DV_D), F32))
    for g in range(n_group):
        slot = g % n_slot if n_group % n_slot == 0 else lax.rem(b * n_group + g, n_slot)
        for cp in page_copies(b, g, slot, False):
            cp.wait()
        tabs = [tab_ref[0]] * (group - 1) + [tab_ref[1 if g == n_group - 1 else 0]]
        scores = [qk(kbuf[slot, p].astype(BF16)) + tabs[p] for p in range(group)]
        values = [vbuf[slot, p].astype(BF16) for p in range(group)]
        state = softmax_update(state, scores, values)
        if g + n_slot < n_group:
            start_group(b, g + n_slot, slot)
        else:
            @pl.when(b + 1 < n_batch)
            def _():
                start_group(b + 1, g + n_slot - n_group, slot)

    n_new = kn_ref.shape[1]
    knew[0:n_new, :] = kn_ref[r]
    vnew[0:n_new, :] = vn_ref[r]
    s_new = qk(knew[...].astype(BF16)) + tab_ref[2]
    _, l, acc = softmax_update(state, [s_new], [vnew[...].astype(BF16)])
    n = acc / l
    o_ref[r] = n - lam_ref[0] * pltpu.roll(n, rows - t_new, 0)


def _inproj_decode_kernel(*refs, rows_per_step, **cfg):
    pt_ref, lam_ref = refs[:2]
    k0 = 2 + _N_INPROJ_IN
    dec_in = refs[k0:k0 + 6]
    k1 = k0 + 6
    o_ref = refs[k1 + _N_INPROJ_OUT]
    scratch = refs[k1 + _N_INPROJ_OUT + 1:]
    _inproj_kernel(*refs[2:k0], *refs[k1:k1 + _N_INPROJ_OUT])
    for r in range(rows_per_step):
        b = pl.program_id(0) * rows_per_step + r

        @pl.when(b < cfg["n_batch"])
        def _():
            _decode_row(b, r, pt_ref, lam_ref, *dec_in, o_ref, *scratch, **cfg)


def _in_projection_with_decode(x2d, g_pre, w_bf16, cos_t, sin_t, tm,
                               page_table, lam, q_rows, k_new, v_new, tabs, cache_k, cache_v,
                               group=16, n_slot=3):
    n_steps = x2d.shape[0] // tm
    n_batch, rows, _ = q_rows.shape
    n_pages = page_table.shape[1]
    width = cache_k.shape[1]
    group = math.gcd(group, n_pages)
    n_slot = min(n_slot, n_pages // group)
    rows_per_step = -(-n_batch // n_steps)
    n_blocks = n_batch // rows_per_step
    assert n_blocks * rows_per_step == n_batch, "sample rows must split evenly over the grid steps"
    per_b = lambda a: pl.BlockSpec((rows_per_step,) + a.shape[1:],
                                   lambda i: (jnp.minimum(i, n_blocks - 1), 0, 0))
    in_specs, out_specs, out_shape = _inproj_specs(x2d, w_bf16, cos_t, tm)
    smem = pl.BlockSpec(memory_space=pltpu.SMEM)
    res = pl.pallas_call(
        functools.partial(_inproj_decode_kernel, rows_per_step=rows_per_step, n_batch=n_batch,
                          n_pages=n_pages, group=group, n_slot=n_slot, t_new=rows // (2 * H_D)),
        grid=(n_steps,),
        in_specs=[smem, smem] + in_specs + [per_b(q_rows), per_b(k_new), per_b(v_new),
                                             pl.BlockSpec(tabs.shape, lambda i: (0, 0, 0)),
                                             pl.BlockSpec(memory_space=pl.ANY),
                                             pl.BlockSpec(memory_space=pl.ANY)],
        out_specs=out_specs + [per_b(q_rows)],
        out_shape=out_shape + [jax.ShapeDtypeStruct(q_rows.shape, F32)],
        scratch_shapes=[pltpu.VMEM((n_slot, group, width, LANES), F32),
                        pltpu.VMEM((n_slot, group, width, LANES), F32),
                        pltpu.VMEM((width, LANES), F32),
                        pltpu.VMEM((width, LANES), F32),
                        pltpu.SemaphoreType.DMA((2, n_slot))],
        compiler_params=_cparams(1),
        name="in_projection_with_decode",
    )(page_table.reshape(-1), lam, x2d, g_pre, w_bf16, cos_t, sin_t, q_rows, k_new, v_new, tabs,
      cache_k, cache_v)
    return res[:_N_INPROJ_OUT], res[_N_INPROJ_OUT]


def _sample_ret_kernel(qt_ref, kt_ref, v_ref, s_ref, o_ref, s_out_ref, *, gamma, bb, t_new):
    for i in range(bb):
        v = v_ref[i]
        for h in range(H_R):
            st = s_ref[i, h]
            qt = qt_ref[i, h]
            kt = kt_ref[i, h]
            outs = []
            for t in range(t_new):
                vrow = v[t:t + 1, h * DV_R:(h + 1) * DV_R]
                st = gamma[h] * st + kt[:, t:t + 1] * vrow
                outs.append(jnp.sum(qt[:, t:t + 1] * st, axis=0, keepdims=True))
            o_ref[i, :, h * DV_R:(h + 1) * DV_R] = jnp.concatenate(outs, axis=0)
            s_out_ref[i, h] = st


def _sample_retention(qr_t, kr_t, vr, state, bb=8):
    n_batch, _, _, t_new = qr_t.shape
    bb = math.gcd(bb, n_batch)
    gamma = tuple(float(v) for v in np.exp(_log_gamma_np()))
    m4 = lambda i: (i, 0, 0, 0)
    m3 = lambda i: (i, 0, 0)
    return pl.pallas_call(
        functools.partial(_sample_ret_kernel, gamma=gamma, bb=bb, t_new=t_new),
        grid=(n_batch // bb,),
        in_specs=[pl.BlockSpec((bb, H_R, DK_R, t_new), m4),
                  pl.BlockSpec((bb, H_R, DK_R, t_new), m4),
                  pl.BlockSpec((bb, t_new, W_R), m3),
                  pl.BlockSpec((bb, H_R, DK_R, DV_R), m4)],
        out_specs=[pl.BlockSpec((bb, t_new, W_R), m3),
                   pl.BlockSpec((bb, H_R, DK_R, DV_R), m4)],
        out_shape=[jax.ShapeDtypeStruct((n_batch, t_new, W_R), F32),
                   jax.ShapeDtypeStruct(state.shape, F32)],
        compiler_params=_cparams(1),
        name="sample_retention",
    )(qr_t, kr_t, vr, state)


def _merge_rows(x, od, orr, gd, gr, gn_d, gn_r, w, g_post):
    def gated(o_all, g_all, gn):
        parts = []
        for h in range(o_all.shape[1] // LANES):
            cols = slice(h * LANES, (h + 1) * LANES)
            o = o_all[:, cols].astype(F32)
            g = g_all[:, cols].astype(F32)
            y = o * lax.rsqrt(jnp.mean(o * o, axis=-1, keepdims=True) + EPS) * gn
            parts.append((y * (g * jax.nn.sigmoid(g))).astype(BF16))
        return parts

    o = jnp.concatenate(gated(od, gd, gn_d) + gated(orr, gr, gn_r), axis=1)
    mix = jnp.dot(o, w, preferred_element_type=F32)
    return x + mix * lax.rsqrt(jnp.mean(mix * mix, axis=-1, keepdims=True) + EPS) * g_post


def _merge_kernel(x_ref, od_ref, or_ref, gd_ref, gr_ref, gnd_ref, gnr_ref, w_ref, gpost_ref, y_ref, *, chain):
    for c in range(x_ref.shape[0] // chain):
        rows = slice(c * chain, (c + 1) * chain)
        y_ref[rows, :] = _merge_rows(x_ref[rows, :], od_ref[rows, :], or_ref[rows, :], gd_ref[rows, :],
                                     gr_ref[rows, :], gnd_ref[...], gnr_ref[...], w_ref[...], gpost_ref[...])


def _ret_merge_kernel(qr_ref, kr_ref, vr_ref, dec_ref, qdec_ref, kdec_ref,
                      x_ref, od_ref, gd_ref, gr_ref, gnd_ref, gnr_ref, w_ref, gpost_ref,
                      y_ref, s_out_ref, st_sc, or_sc, *, g_chunk, bb, rows_per_chain):
    _ret_kernel(qr_ref, kr_ref, vr_ref, dec_ref, qdec_ref, kdec_ref, or_sc, s_out_ref, st_sc,
                g_chunk=g_chunk, bb=bb)
    chunk = x_ref.shape[1]
    for c0 in range(0, bb, rows_per_chain):
        rows = range(c0, c0 + rows_per_chain)
        cat = lambda ref: jnp.concatenate([ref[i] for i in rows], axis=0)
        y = _merge_rows(cat(x_ref), cat(od_ref), cat(or_sc), cat(gd_ref), cat(gr_ref),
                        gnd_ref[...], gnr_ref[...], w_ref[...], gpost_ref[...])
        for k, i in enumerate(rows):
            y_ref[i] = y[k * chunk:(k + 1) * chunk]


def _merge(x2d, od, orr, gd, gr, gn_d, gn_r, w_bf16, g_post, chain, n_chain=2):
    n, d = x2d.shape
    tm = chain * math.gcd(n_chain, n // chain)
    row = lambda i: (i, 0)
    fix = lambda i: (0, 0)
    return pl.pallas_call(
        functools.partial(_merge_kernel, chain=chain),
        grid=(n // tm,),
        in_specs=[pl.BlockSpec((tm, d), row),
                  pl.BlockSpec((tm, W_D), row),
                  pl.BlockSpec((tm, W_R), row),
                  pl.BlockSpec((tm, W_D), row),
                  pl.BlockSpec((tm, W_R), row),
                  pl.BlockSpec((1, DV_D), fix),
                  pl.BlockSpec((1, DV_R), fix),
                  pl.BlockSpec(w_bf16.shape, fix),
                  pl.BlockSpec((1, d), fix)],
        out_specs=pl.BlockSpec((tm, d), row),
        out_shape=jax.ShapeDtypeStruct((n, d), F32),
        compiler_params=_cparams(1),
        name="merge_out_projection",
    )(x2d, od, orr, gd, gr, gn_d, gn_r, w_bf16, g_post)


def _pick_tile(n, target):
    t = math.gcd(n, target)
    assert t % 8 == 0, (n, target)
    return t


def kernel(x_prompt, x_sample, cache_k, cache_v, state_ret, page_table, rel_bias, norm_pre, norm_post,
           w_in, w_out, lambda_q1, lambda_k1, lambda_q2, lambda_k2, gn_diff, gn_ret):
    depth = w_in.shape[0]
    batch, seq, d_model = x_prompt.shape
    dec_batch, t_new, _ = x_sample.shape
    n_pool, page = cache_k.shape[1], cache_k.shape[2]
    n_pages = page_table.shape[1]
    past = n_pages * page
    far = _far_distance()

    t_attn = _pick_tile(seq, 512)
    assert t_attn > far and page > far, "bias must be constant beyond the neighbouring tile / page"
    tm_p = _pick_tile(batch * seq, 512)
    assert seq % tm_p == 0
    tm_s = _pick_tile(dec_batch * t_new, 512)

    vals = jnp.concatenate([(rel_bias - rel_bias[NUM_BUCKETS - 1:]) * LOG2E,
                            jnp.full((1, H_D), NEG, F32)], axis=0).astype(F32)
    tabs_p = _bias_tables(vals, jnp.asarray(_prompt_bucket_codes(t_attn)))
    tabs_s = _bias_tables(vals, jnp.asarray(_decode_bucket_codes(page, t_new)))

    cos_p, sin_p = _rotation_tables(jnp.arange(seq, dtype=jnp.int32))
    pos_s = past + (jnp.arange(tm_s, dtype=jnp.int32) % t_new)
    cos_s, sin_s = _rotation_tables(pos_s)

    yp = x_prompt.reshape(batch * seq, d_model)
    ys = x_sample.reshape(dec_batch * t_new, d_model)
    kp_l, vp_l, sp_l, ks_l, vs_l, ss_l = [], [], [], [], [], []
    for layer in range(depth):
        lambda_init = 0.8 - 0.6 * math.exp(-0.3 * layer)
        lam = (jnp.exp(jnp.sum(lambda_q1[layer].astype(F32) * lambda_k1[layer].astype(F32)))
               - jnp.exp(jnp.sum(lambda_q2[layer].astype(F32) * lambda_k2[layer].astype(F32)))
               + lambda_init).reshape(1).astype(F32)
        w_in_b = w_in[layer].astype(BF16)
        w_out_b = w_out[layer].astype(BF16)
        g_pre = norm_pre[layer].reshape(1, d_model)
        g_post = norm_post[layer].reshape(1, d_model)
        gn_d = (gn_diff[layer].astype(F32) * (1.0 - lambda_init)).reshape(1, DV_D)
        gn_r = gn_ret[layer].astype(F32).reshape(1, DV_R)

        k_s, v_s, _, _, q_s, gd_s, qr_s, kr_s, vr_s, gr_s = _in_projection(
            ys, g_pre, w_in_b, cos_s, sin_s, tm_s)
        q4 = q_s.astype(F32).reshape(dec_batch, t_new, H_D, 2 * D_QK).transpose(0, 2, 1, 3)
        q_rows = jnp.concatenate([q4, q4], axis=2).reshape(dec_batch, H_D * 2 * t_new, 2 * D_QK)

        (k_p, v_p, kb_p, vb_p, q_p, gd_p, qr_p, kr_p, vr_p, gr_p), od_s = _in_projection_with_decode(
            yp, g_pre, w_in_b, cos_p, sin_p, tm_p,
            page_table + layer * n_pool, lam, q_rows,
            k_s.reshape(dec_batch, t_new * H_D, 2 * D_QK),
            v_s.reshape(dec_batch, t_new * H_D, DV_D),
            tabs_s.reshape(3, H_D * 2 * t_new, page * H_D),
            cache_k.reshape(depth * n_pool, page * H_D, 2 * D_QK),
            cache_v.reshape(depth * n_pool, page * H_D, DV_D))
        od_p = _prompt_attention(lam, q_p, kb_p, vb_p, tabs_p, batch, seq, t_attn)
        yp, s_p = _prompt_retention_merge(qr_p, kr_p, vr_p, yp, od_p, gd_p, gr_p, gn_d, gn_r, w_out_b, g_post,
                                          batch, seq)
        kp_l.append(k_p.reshape(batch, seq, H_D, 2 * D_QK))
        vp_l.append(v_p.reshape(batch, seq, H_D, DV_D))
        sp_l.append(s_p)

        od_s = od_s.reshape(dec_batch, H_D, 2, t_new, DV_D)[:, :, 0].transpose(0, 2, 1, 3)
        od_s = od_s.reshape(dec_batch * t_new, W_D)
        to_cols = lambda a: a.astype(F32).reshape(dec_batch, t_new, H_R, DK_R).transpose(0, 2, 3, 1)
        or_s, s_s = _sample_retention(to_cols(qr_s), to_cols(kr_s),
                                      vr_s.astype(F32).reshape(dec_batch, t_new, W_R),
                                      state_ret[layer].astype(F32))
        ys = _merge(ys, od_s, or_s.reshape(dec_batch * t_new, W_R), gd_s, gr_s, gn_d, gn_r, w_out_b, g_post, tm_s)
        ks_l.append(k_s.reshape(dec_batch, t_new, H_D, 2 * D_QK))
        vs_l.append(v_s.reshape(dec_batch, t_new, H_D, DV_D))
        ss_l.append(s_s)

    stack = lambda parts: parts[0][None] if len(parts) == 1 else jnp.stack(parts)
    return (yp.reshape(batch, seq, d_model), ys.reshape(dec_batch, t_new, d_model),
            stack(kp_l), stack(vp_l), stack(sp_l), stack(ks_l), stack(vs_l), stack(ss_l))
```
